```python
import math
import jax
import jax.numpy as jnp
from jax import lax
import numpy as np

D_MODEL = 1024
BATCH = 16
SEQ = 4096
DEPTH = 2
DEC_BATCH = 16
DEC_SEQ = 16
PAST_LEN = 1024

CHUNK = 64
Q_BLOCK = 128
H_A = 8
DH_A = 64
DV_A = 2 * DH_A
W_A = H_A * DV_A
GROUP_CH = 16
N_GROUPS = 48
STATE_P = 64
W_B = N_GROUPS * GROUP_CH
N_MEM = 256
H_M = 4
DH_M = 192
W_M = H_M * DH_M
N_BRANCH = 3
REL_BUCKETS = 32
REL_MAX_DIST = 128
D_FF = 2816
N_EXPERTS = 8
TOP_K = 2
D_FF_EXPERT = 3584
N_DENSE = (DEPTH + 1) // 2
N_MOE = DEPTH // 2
EPS = 1e-6
NEG_INF = -1e30

Q_A_W = H_A * 2 * DH_A
K_A_W = H_A * 2 * DH_A
GATE_W = N_BRANCH * D_MODEL
SPLIT_POINTS = (Q_A_W, Q_A_W + K_A_W, Q_A_W + K_A_W + W_A, Q_A_W + K_A_W + W_A + W_B,
                Q_A_W + K_A_W + W_A + W_B + W_M)
W_IN = Q_A_W + K_A_W + W_A + W_B + W_M + GATE_W

kernel_name = "hybrid_diffattn_s5_memxattn_stream_step"


def rmsnorm(x, g):
    xf = x.astype(jnp.float32)
    y = xf * lax.rsqrt(jnp.mean(xf * xf, axis=-1, keepdims=True) + EPS)
    return (y * g.astype(jnp.float32)).astype(x.dtype)


def rel_pos_bias(q_pos, k_pos, table):
    rel = k_pos[None, :] - q_pos[:, None]
    half = REL_BUCKETS // 2
    max_exact = half // 2
    n = jnp.abs(rel)
    nf = jnp.maximum(n, 1).astype(jnp.float32)
    large = max_exact + (jnp.log(nf / max_exact) / math.log(REL_MAX_DIST / max_exact)
                         * (half - max_exact)).astype(jnp.int32)
    large = jnp.minimum(large, half - 1)
    bucket = jnp.where(rel > 0, half, 0) + jnp.where(n < max_exact, n, large)
    return jnp.transpose(table[bucket].astype(jnp.float32), (2, 0, 1))


def diff_attend(q, k, v, q_pos, k_pos, rel_table, lam):
    logits = jnp.einsum("bqhcd,bkhcd->bhcqk", q, k).astype(jnp.float32) * (DH_A ** -0.5)
    bias = rel_pos_bias(q_pos, k_pos, rel_table)
    visible = (k_pos[None, :] // CHUNK) <= (q_pos[:, None] // CHUNK)
    logits = jnp.where(visible, logits + bias[None, :, None], NEG_INF)
    probs = jax.nn.softmax(logits, axis=-1)
    weights = probs[:, :, 0] - lam * probs[:, :, 1]
    return jnp.einsum("bhqk,bkhe->bqhe", weights.astype(v.dtype), v)


def _ssm_combine(e1, e2):
    a1r, a1i, b1r, b1i = e1
    a2r, a2i, b2r, b2i = e2
    return (a2r * a1r - a2i * a1i, a2r * a1i + a2i * a1r,
            a2r * b1r - a2i * b1i + b2r, a2r * b1i + a2i * b1r + b2i)


def s5_scan(u, p, s0_re=None, s0_im=None):
    bsz, l, _ = u.shape
    f32 = jnp.float32
    uf = u.astype(f32).reshape(bsz, l, N_GROUPS, GROUP_CH)
    dt = jnp.exp(p["log_dt"].astype(f32))[:, None]
    lr = jnp.minimum(p["lambda_re"].astype(f32), -1e-4)
    lim = p["lambda_im"].astype(f32)
    mag = jnp.exp(lr * dt)
    ar = mag * jnp.cos(lim * dt)
    ai = mag * jnp.sin(lim * dt)
    den = lr * lr + lim * lim
    fr = ((ar - 1.0) * lr + ai * lim) / den
    fi = (ai * lr - (ar - 1.0) * lim) / den
    br = p["b_re"].astype(f32)
    bi = p["b_im"].astype(f32)
    bbr = fr[..., None] * br - fi[..., None] * bi
    bbi = fr[..., None] * bi + fi[..., None] * br
    bu_r = jnp.einsum("blgc,gpc->blgp", uf, bbr)
    bu_i = jnp.einsum("blgc,gpc->blgp", uf, bbi)
    a_r = jnp.broadcast_to(ar, (1, l, N_GROUPS, STATE_P))
    a_i = jnp.broadcast_to(ai, (1, l, N_GROUPS, STATE_P))
    pr, pim, xr, xi = lax.associative_scan(_ssm_combine, (a_r, a_i, bu_r, bu_i), axis=1)
    if s0_re is not None:
        s_r = s0_re.astype(f32)[:, None]
        s_i = s0_im.astype(f32)[:, None]
        xr, xi = xr + pr * s_r - pim * s_i, xi + pr * s_i + pim * s_r
    y = (jnp.einsum("blgp,gcp->blgc", xr, p["c_re"].astype(f32))
         - jnp.einsum("blgp,gcp->blgc", xi, p["c_im"].astype(f32))
         + p["d_skip"].astype(f32) * uf).reshape(bsz, l, W_B)
    yg = jax.nn.gelu(y)
    out = yg * jax.nn.sigmoid(yg @ p["w_glu"].astype(f32) + p["b_glu"].astype(f32))
    return out.astype(u.dtype), xr[:, -1].astype(u.dtype), xi[:, -1].astype(u.dtype)


def memory_kv(mem, p):
    b, m, _ = mem.shape
    mk = rmsnorm((mem @ p["w_mk"]).reshape(b, m, H_M, DH_M), p["k_norm_m"])
    mv = (mem @ p["w_mv"]).reshape(b, m, H_M, DH_M)
    return mk, mv


def token_mixers(h, p, layer_idx, rel_table, mem_k, mem_v, k_past=None, v_past=None, s0_re=None, s0_im=None):
    b, l, _ = h.shape
    f32 = jnp.float32
    proj = h @ p["w_in"]
    q_a, k_a, v_a, u_b, q_m, gate_pre = jnp.split(proj, SPLIT_POINTS, axis=-1)
    q_a = rmsnorm(q_a.reshape(b, l, H_A, 2, DH_A), p["q_norm_a"])
    k_a = rmsnorm(k_a.reshape(b, l, H_A, 2, DH_A), p["k_norm_a"])
    v_a = v_a.reshape(b, l, H_A, DV_A)
    lam_init = 0.8 - 0.6 * math.exp(-0.3 * layer_idx)
    lam = (jnp.exp(jnp.sum(p["lam_q1"].astype(f32) * p["lam_k1"].astype(f32)))
           - jnp.exp(jnp.sum(p["lam_q2"].astype(f32) * p["lam_k2"].astype(f32))) + lam_init)
    if k_past is None:
        pos = jnp.arange(l)
        outs = []
        for start in range(0, l, Q_BLOCK):
            end = start + Q_BLOCK
            outs.append(diff_attend(q_a[:, start:end], k_a[:, :end], v_a[:, :end],
                                    pos[start:end], pos[:end], rel_table, lam))
        o_a = jnp.concatenate(outs, axis=1)
    else:
        past = k_past.shape[1]
        k_all = jnp.concatenate([k_past.reshape(b, past, H_A, 2, DH_A), k_a], axis=1)
        v_all = jnp.concatenate([v_past, v_a], axis=1)
        k_pos = jnp.arange(past + l)
        o_a = diff_attend(q_a, k_all, v_all, k_pos[past:], k_pos, rel_table, lam)
    o_a = rmsnorm(o_a, p["subln_a"]) * (1.0 - lam_init)
    p_a = o_a.reshape(b, l, W_A) @ p["w_oa"]
    y_b, s_re, s_im = s5_scan(u_b, p, s0_re, s0_im)
    p_b = y_b @ p["w_ob"]
    q_m = rmsnorm(q_m.reshape(b, l, H_M, DH_M), p["q_norm_m"])
    logits_m = jnp.einsum("bqhd,bmhd->bhqm", q_m, mem_k).astype(f32) * (DH_M ** -0.5)
    probs_m = jax.nn.softmax(logits_m, axis=-1)
    o_m = jnp.einsum("bhqm,bmhd->bqhd", probs_m.astype(mem_v.dtype), mem_v).reshape(b, l, W_M)
    p_m = o_m @ p["w_om"]
    g_a, g_b, g_m = jnp.split(jax.nn.sigmoid(gate_pre), N_BRANCH, axis=-1)
    out = (g_a * p_a + g_b * p_b + g_m * p_m) @ p["w_out"]
    return out, k_a.reshape(b, l, H_A, 2 * DH_A), v_a, s_re, s_im


def swiglu(h, wg, wu, wd):
    return (jax.nn.silu(h @ wg) * (h @ wu)) @ wd


def moe_swiglu(h, router, wg, wu, wd):
    b, l, d = h.shape
    t = h.reshape(b * l, d)
    logits = (t @ router).astype(jnp.float32)
    top_v, top_i = lax.top_k(logits, TOP_K)
    gates = jax.nn.softmax(top_v, axis=-1)
    combine = jnp.sum(jax.nn.one_hot(top_i, N_EXPERTS, dtype=jnp.float32) * gates[..., None], axis=1)
    out = jnp.zeros_like(t)
    for e in range(N_EXPERTS):
        out = out + combine[:, e:e + 1].astype(t.dtype) * swiglu(t, wg[e], wu[e], wd[e])
    return out.reshape(b, l, d)


def channel_mixer(h, li, ffn_w_gate, ffn_w_up, ffn_w_down, router, moe_w_gate, moe_w_up, moe_w_down):
    j = li // 2
    if li % 2 == 0:
        return swiglu(h, ffn_w_gate[j], ffn_w_up[j], ffn_w_down[j])
    return moe_swiglu(h, router[j], moe_w_gate[j], moe_w_up[j], moe_w_down[j])


def setup_inputs(seed: int = 0) -> dict:
    key = jax.random.key(seed)
    keys = iter(jax.random.split(key, 64))

    def nrm(shape, scale=1.0):
        return jax.random.normal(next(keys), shape, jnp.float32) * scale

    def gain(shape):
        return 1.0 + nrm(shape, 0.02)

    L = DEPTH
    n_idx = jnp.arange(STATE_P, dtype=jnp.float32)
    return {
        "x_prompt": nrm((BATCH, SEQ, D_MODEL)),
        "x_sample": nrm((DEC_BATCH, DEC_SEQ, D_MODEL)),
        "mem_prompt": nrm((BATCH, N_MEM, D_MODEL)),
        "cache_attn_k": nrm((L, DEC_BATCH, PAST_LEN, H_A, 2 * DH_A)),
        "cache_attn_v": nrm((L, DEC_BATCH, PAST_LEN, H_A, DV_A)),
        "state_ssm_re": nrm((L, DEC_BATCH, N_GROUPS, STATE_P)),
        "state_ssm_im": nrm((L, DEC_BATCH, N_GROUPS, STATE_P)),
        "cache_mem_k": nrm((L, DEC_BATCH, N_MEM, H_M, DH_M)),
        "cache_mem_v": nrm((L, DEC_BATCH, N_MEM, H_M, DH_M)),
        "norm1": gain((L, D_MODEL)),
        "norm2": gain((L, D_MODEL)),
        "w_in": nrm((L, D_MODEL, W_IN), D_MODEL ** -0.5),
        "q_norm_a": gain((L, DH_A)),
        "k_norm_a": gain((L, DH_A)),
        "lam_q1": nrm((L, DH_A), 0.1),
        "lam_k1": nrm((L, DH_A), 0.1),
        "lam_q2": nrm((L, DH_A), 0.1),
        "lam_k2": nrm((L, DH_A), 0.1),
        "subln_a": gain((L, DV_A)),
        "w_oa": nrm((L, W_A, D_MODEL), W_A ** -0.5),
        "lambda_re": -0.5 + nrm((L, N_GROUPS, STATE_P), 0.01),
        "lambda_im": math.pi * n_idx + nrm((L, N_GROUPS, STATE_P), 0.01),
        "log_dt": jax.random.uniform(next(keys), (L, N_GROUPS), jnp.float32, math.log(1e-3), math.log(1e-1)),
        "b_re": nrm((L, N_GROUPS, STATE_P, GROUP_CH), (2 * GROUP_CH) ** -0.5),
        "b_im": nrm((L, N_GROUPS, STATE_P, GROUP_CH), (2 * GROUP_CH) ** -0.5),
        "c_re": nrm((L, N_GROUPS, GROUP_CH, STATE_P), STATE_P ** -0.5),
        "c_im": nrm((L, N_GROUPS, GROUP_CH, STATE_P), STATE_P ** -0.5),
        "d_skip": nrm((L, N_GROUPS, GROUP_CH)),
        "w_glu": nrm((L, W_B, W_B), W_B ** -0.5),
        "b_glu": nrm((L, W_B), 0.01),
        "w_ob": nrm((L, W_B, D_MODEL), W_B ** -0.5),
        "w_mk": nrm((L, D_MODEL, W_M), D_MODEL ** -0.5),
        "w_mv": nrm((L, D_MODEL, W_M), D_MODEL ** -0.5),
        "q_norm_m": gain((L, DH_M)),
        "k_norm_m": gain((L, DH_M)),
        "w_om": nrm((L, W_M, D_MODEL), W_M ** -0.5),
        "w_out": nrm((L, D_MODEL, D_MODEL), D_MODEL ** -0.5),
        "rel_bias": nrm((REL_BUCKETS, H_A), 0.5),
        "ffn_w_gate": nrm((N_DENSE, D_MODEL, D_FF), D_MODEL ** -0.5),
        "ffn_w_up": nrm((N_DENSE, D_MODEL, D_FF), D_MODEL ** -0.5),
        "ffn_w_down": nrm((N_DENSE, D_FF, D_MODEL), D_FF ** -0.5),
        "router": nrm((N_MOE, D_MODEL, N_EXPERTS), D_MODEL ** -0.5),
        "moe_w_gate": nrm((N_MOE, N_EXPERTS, D_MODEL, D_FF_EXPERT), D_MODEL ** -0.5),
        "moe_w_up": nrm((N_MOE, N_EXPERTS, D_MODEL, D_FF_EXPERT), D_MODEL ** -0.5),
        "moe_w_down": nrm((N_MOE, N_EXPERTS, D_FF_EXPERT, D_MODEL), D_FF_EXPERT ** -0.5),
    }


def reference(x_prompt, x_sample, mem_prompt, cache_attn_k, cache_attn_v, state_ssm_re, state_ssm_im,
              cache_mem_k, cache_mem_v, norm1, norm2, w_in, q_norm_a, k_norm_a, lam_q1, lam_k1, lam_q2,
              lam_k2, subln_a, w_oa, lambda_re, lambda_im, log_dt, b_re, b_im, c_re, c_im, d_skip, w_glu,
              b_glu, w_ob, w_mk, w_mv, q_norm_m, k_norm_m, w_om, w_out, rel_bias, ffn_w_gate, ffn_w_up,
              ffn_w_down, router, moe_w_gate, moe_w_up, moe_w_down):
    xp = x_prompt
    xs = x_sample
    kp_list, vp_list, srp_list, sip_list, mkp_list, mvp_list = [], [], [], [], [], []
    ks_list, vs_list, srs_list, sis_list = [], [], [], []
    for li in range(DEPTH):
        p = {
            "w_in": w_in[li], "q_norm_a": q_norm_a[li], "k_norm_a": k_norm_a[li],
            "lam_q1": lam_q1[li], "lam_k1": lam_k1[li], "lam_q2": lam_q2[li], "lam_k2": lam_k2[li],
            "subln_a": subln_a[li], "w_oa": w_oa[li],
            "lambda_re": lambda_re[li], "lambda_im": lambda_im[li], "log_dt": log_dt[li],
            "b_re": b_re[li], "b_im": b_im[li], "c_re": c_re[li], "c_im": c_im[li],
            "d_skip": d_skip[li], "w_glu": w_glu[li], "b_glu": b_glu[li], "w_ob": w_ob[li],
            "w_mk": w_mk[li], "w_mv": w_mv[li], "q_norm_m": q_norm_m[li], "k_norm_m": k_norm_m[li],
            "w_om": w_om[li], "w_out": w_out[li],
        }
        mk_p, mv_p = memory_kv(mem_prompt, p)
        mix_p, k_p, v_p, sr_p, si_p = token_mixers(rmsnorm(xp, norm1[li]), p, li, rel_bias, mk_p, mv_p)
        xp = xp + mix_p
        xp = xp + channel_mixer(rmsnorm(xp, norm2[li]), li, ffn_w_gate, ffn_w_up, ffn_w_down,
                                router, moe_w_gate, moe_w_up, moe_w_down)
        mix_s, k_s, v_s, sr_s, si_s = token_mixers(rmsnorm(xs, norm1[li]), p, li, rel_bias,
                                                   cache_mem_k[li], cache_mem_v[li],
                                                   cache_attn_k[li], cache_attn_v[li],
                                                   state_ssm_re[li], state_ssm_im[li])
        xs = xs + mix_s
        xs = xs + channel_mixer(rmsnorm(xs, norm2[li]), li, ffn_w_gate, ffn_w_up, ffn_w_down,
                                router, moe_w_gate, moe_w_up, moe_w_down)
        kp_list.append(k_p)
        vp_list.append(v_p)
        srp_list.append(sr_p)
        sip_list.append(si_p)
        mkp_list.append(mk_p)
        mvp_list.append(mv_p)
        ks_list.append(k_s)
        vs_list.append(v_s)
        srs_list.append(sr_s)
        sis_list.append(si_s)
    attn_k_prompt = jnp.stack(kp_list)
    attn_v_prompt = jnp.stack(vp_list)
    ssm_re_prompt = jnp.stack(srp_list)
    ssm_im_prompt = jnp.stack(sip_list)
    mem_k_prompt = jnp.stack(mkp_list)
    mem_v_prompt = jnp.stack(mvp_list)
    attn_k_sample = jnp.stack(ks_list)
    attn_v_sample = jnp.stack(vs_list)
    ssm_re_sample = jnp.stack(srs_list)
    ssm_im_sample = jnp.stack(sis_list)
    return (xp, xs, attn_k_prompt, attn_v_prompt, ssm_re_prompt, ssm_im_prompt, mem_k_prompt, mem_v_prompt,
            attn_k_sample, attn_v_sample, ssm_re_sample, ssm_im_sample)
```

```python
import functools
import math

import jax
import jax.numpy as jnp
from jax import lax
from jax.experimental import pallas as pl
from jax.experimental.pallas import tpu as pltpu

F32 = jnp.float32
BF16 = jnp.bfloat16
I32 = jnp.int32

D_MODEL = 1024
CHUNK = 64
H_A = 8
DH_A = 64
DV_A = 2 * DH_A
W_A = H_A * DV_A
GROUP_CH = 16
N_GROUPS = 48
STATE_P = 64
W_B = N_GROUPS * GROUP_CH
H_M = 4
DH_M = 192
W_M = H_M * DH_M
REL_BUCKETS = 32
REL_MAX_DIST = 128
N_EXPERTS = 8
EPS = 1e-6
NEG_INF = -1e30
Q_A_W = H_A * 2 * DH_A
SPLITS = (0, Q_A_W, 2 * Q_A_W, 2 * Q_A_W + W_A, 2 * Q_A_W + W_A + W_B, 2 * Q_A_W + W_A + W_B + W_M)
GATE_W = 3 * D_MODEL

LANES = 128
VMEM_LIMIT = 56 * 1024 * 1024
S5_SLAB_GROUPS = LANES // GROUP_CH
S5_SLABS = N_GROUPS // S5_SLAB_GROUPS
S5_SLAB_STATE = S5_SLAB_GROUPS * STATE_P


def _cparams(*sem):
    return pltpu.CompilerParams(dimension_semantics=sem, vmem_limit_bytes=VMEM_LIMIT)


def _rms_kernel(x_ref, g_ref, o_ref):
    x = x_ref[...]
    ms = jnp.mean(x * x, axis=-1, keepdims=True)
    o_ref[...] = ((x * lax.rsqrt(ms + EPS)) * g_ref[...]).astype(o_ref.dtype)


def _rms(x, g):
    t, d = x.shape
    tm = min(1024, t)
    return pl.pallas_call(
        _rms_kernel,
        grid=(t // tm,),
        in_specs=[pl.BlockSpec((tm, d), lambda i: (i, 0)), pl.BlockSpec((1, d), lambda i: (0, 0))],
        out_specs=pl.BlockSpec((tm, d), lambda i: (i, 0)),
        out_shape=jax.ShapeDtypeStruct((t, d), BF16),
        compiler_params=_cparams("parallel"),
        name="rmsnorm",
    )(x, g.reshape(1, d).astype(F32))


def _seg64_scale(ys):
    lane = lax.broadcasted_iota(I32, (1, LANES), 1)
    left = lane < DH_A
    y2 = ys * ys
    sl = jnp.sum(jnp.where(left, y2, 0.0), axis=-1, keepdims=True)
    sr = jnp.sum(jnp.where(left, 0.0, y2), axis=-1, keepdims=True)
    rl = lax.rsqrt(sl * (1.0 / DH_A) + EPS)
    rr = lax.rsqrt(sr * (1.0 / DH_A) + EPS)
    return jnp.where(left, rl, rr)


def _seg192_scale(y):
    col = lax.broadcasted_iota(I32, (1, W_M), 1)
    y2 = y * y
    rb = jnp.zeros_like(y)
    for h in range(H_M):
        m = (col >= DH_M * h) & (col < DH_M * (h + 1))
        s = jnp.sum(jnp.where(m, y2, 0.0), axis=-1, keepdims=True)
        rb = jnp.where(m, lax.rsqrt(s / DH_M + EPS), rb)
    return rb


def _mm_kernel(*refs, epi, n_out, has_gain):
    a_ref, w_ref = refs[0], refs[1]
    g_ref = refs[2] if has_gain else None
    outs = refs[2 + int(has_gain):]
    assert len(outs) == n_out
    y = jnp.dot(a_ref[...], w_ref[...], preferred_element_type=F32)
    tn = y.shape[1]
    if epi == "seg64":
        for s in range(tn // LANES):
            sl = slice(s * LANES, (s + 1) * LANES)
            ys = y[:, sl]
            r = (ys * _seg64_scale(ys)) * g_ref[:, sl]
            for o in outs:
                o[:, sl] = r.astype(o.dtype)
        return
    if epi == "seg192":
        y = (y * _seg192_scale(y)) * g_ref[...]
    elif epi == "sigmoid":
        y = jax.nn.sigmoid(y)
    for o in outs:
        o[...] = y.astype(o.dtype)


def _mm(a, w, *, epi="plain", gain=None, out_dtypes=(BF16,), tn=None, time_major=None):
    t, k = a.shape
    n = w.shape[1]
    tn = n if tn is None else tn
    assert n % tn == 0
    if time_major is None:
        tm = min(1024, t)
        grid = (t // tm, n // tn)
        out_map = lambda i, j: (i, j)
        out_shapes = [jax.ShapeDtypeStruct((t, n), dt) for dt in out_dtypes]
    else:
        b, l = time_major
        tm = min(1024, l)
        assert tn == n and l % tm == 0
        nl = l // tm
        grid = (t // tm, 1)
        out_map = lambda i, j: (i % nl, i // nl)
        out_shapes = [jax.ShapeDtypeStruct((l, b * n), dt) for dt in out_dtypes]
    assert t % tm == 0
    in_specs = [pl.BlockSpec((tm, k), lambda i, j: (i, 0)), pl.BlockSpec((k, tn), lambda i, j: (0, j))]
    args = [a, w]
    if gain is not None:
        in_specs.append(pl.BlockSpec((1, tn), lambda i, j: (0, j)))
        args.append(gain.reshape(1, n).astype(F32))
    res = pl.pallas_call(
        functools.partial(_mm_kernel, epi=epi, n_out=len(out_dtypes), has_gain=gain is not None),
        grid=grid,
        in_specs=in_specs,
        out_specs=[pl.BlockSpec((tm, tn), out_map) for _ in out_dtypes],
        out_shape=out_shapes,
        compiler_params=_cparams("parallel", "parallel"),
        name="mm_" + epi,
    )(*args)
    return res


def _rel_bias(q_pos, k_pos, table):
    rel = k_pos[None, :] - q_pos[:, None]
    half = REL_BUCKETS // 2
    max_exact = half // 2
    n = jnp.abs(rel)
    nf = jnp.maximum(n, 1).astype(F32)
    large = max_exact + (jnp.log(nf / max_exact) / math.log(REL_MAX_DIST / max_exact)
                         * (half - max_exact)).astype(I32)
    large = jnp.minimum(large, half - 1)
    bucket = jnp.where(rel > 0, half, 0) + jnp.where(n < max_exact, n, large)
    bias = jnp.transpose(table[bucket].astype(F32), (2, 0, 1))
    visible = (k_pos[None, :] // CHUNK) <= (q_pos[:, None] // CHUNK)
    return jnp.where(visible[None], bias, NEG_INF)


def _split_halves(q):
    lane = lax.broadcasted_iota(I32, (1, LANES), 1)
    zero = jnp.zeros_like(q)
    return jnp.concatenate([jnp.where(lane < DH_A, q, zero), jnp.where(lane >= DH_A, q, zero)], axis=0)


def _subln(o, g, lam_init):
    ms = jnp.mean(o * o, axis=-1, keepdims=True)
    return ((o * lax.rsqrt(ms + EPS)) * g) * (1.0 - lam_init)


def _attn_prompt_kernel(lam_ref, cfar_ref, q_ref, k_ref, v_ref, bias_ref, g_ref, o_ref,
                        m_sc, l_sc, acc_sc, *, tq, lam_init):
    h = pl.program_id(1)
    qi = pl.program_id(2)
    qs = _split_halves(q_ref[...])
    m_sc[...] = jnp.full(m_sc.shape, -jnp.inf, F32)
    l_sc[...] = jnp.zeros(l_sc.shape, F32)
    acc_sc[...] = jnp.zeros(acc_sc.shape, F32)

    def step(j, bias):
        r0 = pl.multiple_of(j * tq, tq)
        kb = k_ref[pl.ds(r0, tq), :]
        vb = v_ref[pl.ds(r0, tq), :]
        s = lax.dot_general(qs, kb, (((1,), (1,)), ((), ())), preferred_element_type=F32)
        s = s + bias
        m_old = m_sc[...]
        m_new = jnp.maximum(m_old, jnp.max(s, axis=-1, keepdims=True))
        alpha = jnp.exp(m_old - m_new)
        p = jnp.exp(s - m_new)
        l_sc[...] = alpha * l_sc[...] + jnp.sum(p, axis=-1, keepdims=True)
        acc_sc[...] = alpha * acc_sc[...] + jnp.dot(p.astype(BF16), vb, preferred_element_type=F32)
        m_sc[...] = m_new

    cfar = cfar_ref[h]

    def far_body(j, c):
        step(j, cfar)
        return c

    lax.fori_loop(0, jnp.maximum(qi - 1, 0), far_body, 0)

    @pl.when(qi >= 1)
    def _():
        b = bias_ref[0]
        step(qi - 1, jnp.concatenate([b, b], axis=0))

    bd = bias_ref[1]
    step(qi, jnp.concatenate([bd, bd], axis=0))

    o = acc_sc[...] / l_sc[...]
    o = o[:tq] - lam_ref[0] * o[tq:]
    o_ref[...] = _subln(o, g_ref[...], lam_init).astype(o_ref.dtype)


def _attn_prompt(q, k, v, rel_table, subln_g, lam, lam_init, b, l):
    tq = min(256, l)
    assert l % tq == 0 and tq % CHUNK == 0 and tq >= 128
    pos_q = jnp.arange(tq, 2 * tq)
    bias = _rel_bias(pos_q, jnp.arange(2 * tq), rel_table)
    bias = jnp.stack([bias[:, :, :tq], bias[:, :, tq:]], axis=1)
    cfar = _rel_bias(jnp.array([2 * tq + 1]), jnp.array([0]), rel_table)[:, 0, 0]
    q3, k3, v3 = (z.reshape(b, l, W_A) for z in (q, k, v))
    smem = pl.BlockSpec(memory_space=pltpu.SMEM)
    out = pl.pallas_call(
        functools.partial(_attn_prompt_kernel, tq=tq, lam_init=lam_init),
        grid=(b, H_A, l // tq),
        in_specs=[smem, smem,
                  pl.BlockSpec((None, tq, DV_A), lambda bi, h, i: (bi, i, h)),
                  pl.BlockSpec((None, l, DV_A), lambda bi, h, i: (bi, 0, h)),
                  pl.BlockSpec((None, l, DV_A), lambda bi, h, i: (bi, 0, h)),
                  pl.BlockSpec((None, 2, tq, tq), lambda bi, h, i: (h, 0, 0, 0)),
                  pl.BlockSpec((1, DV_A), lambda bi, h, i: (0, 0))],
        out_specs=pl.BlockSpec((None, tq, DV_A), lambda bi, h, i: (bi, i, h)),
        out_shape=jax.ShapeDtypeStruct((b, l, W_A), BF16),
        scratch_shapes=[pltpu.VMEM((2 * tq, 1), F32), pltpu.VMEM((2 * tq, 1), F32),
                        pltpu.VMEM((2 * tq, DV_A), F32)],
        compiler_params=_cparams("parallel", "parallel", "parallel"),
        name="attn_prompt",
    )(lam.reshape(1), cfar, q3, k3, v3, bias, subln_g.reshape(1, DV_A).astype(F32))
    return out.reshape(b * l, W_A)


def _attn_sample_kernel(lam_ref, q_ref, kp_ref, vp_ref, kn_ref, vn_ref, bp_ref, bn_ref, g_ref, o_ref,
                        *, lq, lam_init):
    lam = lam_ref[0]
    for h in range(H_A):
        sl = slice(h * DV_A, (h + 1) * DV_A)
        qs = _split_halves(q_ref[:, sl])
        kp = kp_ref[:, sl].astype(BF16)
        vp = vp_ref[:, sl].astype(BF16)
        kn = kn_ref[:, sl]
        vn = vn_ref[:, sl]
        nt = (((1,), (1,)), ((), ()))
        bp = bp_ref[h]
        bn = bn_ref[h]
        sp = lax.dot_general(qs, kp, nt, preferred_element_type=F32) + jnp.concatenate([bp, bp], axis=0)
        sn = lax.dot_general(qs, kn, nt, preferred_element_type=F32) + jnp.concatenate([bn, bn], axis=0)
        m = jnp.maximum(jnp.max(sp, axis=-1, keepdims=True), jnp.max(sn, axis=-1, keepdims=True))
        pp = jnp.exp(sp - m)
        pn = jnp.exp(sn - m)
        lsum = jnp.sum(pp, axis=-1, keepdims=True) + jnp.sum(pn, axis=-1, keepdims=True)
        o = (jnp.dot(pp.astype(BF16), vp, preferred_element_type=F32)
             + jnp.dot(pn.astype(BF16), vn, preferred_element_type=F32)) / lsum
        o = o[:lq] - lam * o[lq:]
        o_ref[:, sl] = _subln(o, g_ref[...], lam_init).astype(o_ref.dtype)


def _attn_sample(q, k_new, v_new, k_past, v_past, rel_table, subln_g, lam, lam_init, b, l):
    past = k_past.shape[1]
    k_pos = jnp.arange(past + l)
    bias = _rel_bias(k_pos[past:], k_pos, rel_table)
    bp, bn = bias[:, :, :past], bias[:, :, past:]
    smem = pl.BlockSpec(memory_space=pltpu.SMEM)
    blk = lambda rows: pl.BlockSpec((None, rows, W_A), lambda bi: (bi, 0, 0))
    full = lambda shape: pl.BlockSpec(shape, lambda bi: tuple(0 for _ in shape))
    out = pl.pallas_call(
        functools.partial(_attn_sample_kernel, lq=l, lam_init=lam_init),
        grid=(b,),
        in_specs=[smem, blk(l), blk(past), blk(past), blk(l), blk(l),
                  full((H_A, l, past)), full((H_A, l, l)), full((1, DV_A))],
        out_specs=blk(l),
        out_shape=jax.ShapeDtypeStruct((b, l, W_A), BF16),
        compiler_params=_cparams("parallel"),
        name="attn_sample",
    )(lam.reshape(1), q.reshape(b, l, W_A), k_past.reshape(b, past, W_A), v_past.reshape(b, past, W_A),
      k_new.reshape(b, l, W_A), v_new.reshape(b, l, W_A), bp, bn, subln_g.reshape(1, DV_A).astype(F32))
    return out.reshape(b * l, W_A)


def _memattn_kernel(q_ref, mk_ref, mv_ref, o_ref, *, tq):
    col = lax.broadcasted_iota(I32, (1, W_M), 1)
    q = q_ref[...]
    zero = jnp.zeros_like(q)
    masks = [(col >= DH_M * h) & (col < DH_M * (h + 1)) for h in range(H_M)]
    qs = jnp.concatenate([jnp.where(m, q, zero) for m in masks], axis=0)
    s = lax.dot_general(qs, mk_ref[...], (((1,), (1,)), ((), ())), preferred_element_type=F32)
    m = jnp.max(s, axis=-1, keepdims=True)
    p = jnp.exp(s - m)
    p = p / jnp.sum(p, axis=-1, keepdims=True)
    o_all = jnp.dot(p.astype(BF16), mv_ref[...], preferred_element_type=F32)
    o = jnp.zeros((tq, W_M), F32)
    for h in range(H_M):
        o = jnp.where(masks[h], o_all[h * tq:(h + 1) * tq], o)
    o_ref[...] = o.astype(o_ref.dtype)


def _memattn(qm, mk, mv, b, l):
    n_mem = mk.shape[1]
    tq = min(256, l)
    assert l % tq == 0
    out = pl.pallas_call(
        functools.partial(_memattn_kernel, tq=tq),
        grid=(b, l // tq),
        in_specs=[pl.BlockSpec((None, tq, W_M), lambda bi, i: (bi, i, 0)),
                  pl.BlockSpec((None, n_mem, W_M), lambda bi, i: (bi, 0, 0)),
                  pl.BlockSpec((None, n_mem, W_M), lambda bi, i: (bi, 0, 0))],
        out_specs=pl.BlockSpec((None, tq, W_M), lambda bi, i: (bi, i, 0)),
        out_shape=jax.ShapeDtypeStruct((b, l, W_M), BF16),
        compiler_params=_cparams("parallel", "parallel"),
        name="memattn",
    )(qm.reshape(b, l, W_M), mk, mv)
    return out.reshape(b * l, W_M)


def _s5_tables(p):
    dt = jnp.exp(p["log_dt"].astype(F32))[:, None]
    lr = jnp.minimum(p["lambda_re"].astype(F32), -1e-4)
    lim = p["lambda_im"].astype(F32)
    mag = jnp.exp(lr * dt)
    ar = mag * jnp.cos(lim * dt)
    ai = mag * jnp.sin(lim * dt)
    den = lr * lr + lim * lim
    fr = ((ar - 1.0) * lr + ai * lim) / den
    fi = (ai * lr - (ar - 1.0) * lim) / den
    br = p["b_re"].astype(F32)
    bi = p["b_im"].astype(F32)
    bbr = fr[..., None] * br - fi[..., None] * bi
    bbi = fr[..., None] * bi + fi[..., None] * br
    eye = jnp.eye(S5_SLAB_GROUPS, dtype=F32)
    sg = (S5_SLABS, S5_SLAB_GROUPS)

    def in_w(bb):
        w = jnp.einsum("sgpc,gh->sgchp", bb.reshape(*sg, STATE_P, GROUP_CH), eye)
        return w.reshape(S5_SLABS, LANES, S5_SLAB_STATE)

    def out_w(c):
        w = jnp.einsum("sgcp,gh->sgphc", c.reshape(*sg, GROUP_CH, STATE_P), eye)
        return w.reshape(S5_SLABS, S5_SLAB_STATE, LANES)

    wb = jnp.concatenate([in_w(bbr), in_w(bbi)], axis=2).astype(BF16)
    wc = jnp.concatenate([out_w(p["c_re"].astype(F32)), -out_w(p["c_im"].astype(F32))], axis=1).astype(BF16)
    a_re = ar.reshape(S5_SLABS, 1, S5_SLAB_STATE)
    a_im = ai.reshape(S5_SLABS, 1, S5_SLAB_STATE)
    return wb, wc, a_re, a_im


def _s5_kernel(u_ref, wb_ref, wc_ref, ar_ref, ai_ref, d_ref, wglu_ref, bglu_ref, s0r_ref, s0i_ref,
               y_ref, sr_ref, si_ref, bu_sc, x_sc, y_sc, *, tl, nb):
    i = pl.program_id(0)

    @pl.when(i == 0)
    def _():
        sr_ref[...] = s0r_ref[...]
        si_ref[...] = s0i_ref[...]

    rows = tl * nb
    u = u_ref[...].reshape(rows, W_B)
    y_sc[...] = d_ref[...] * u
    for s in range(S5_SLABS):
        sl = slice(s * LANES, (s + 1) * LANES)
        bu_sc[...] = jnp.dot(u[:, sl].astype(BF16), wb_ref[s], preferred_element_type=F32)
        ar = jnp.broadcast_to(ar_ref[s], (nb, S5_SLAB_STATE))
        ai = jnp.broadcast_to(ai_ref[s], (nb, S5_SLAB_STATE))

        def body(t, carry):
            sr, si = carry
            r0 = pl.multiple_of(t * nb, nb)
            b_r = bu_sc[pl.ds(r0, nb), :S5_SLAB_STATE]
            b_i = bu_sc[pl.ds(r0, nb), S5_SLAB_STATE:]
            nr = ar * sr - ai * si + b_r
            ni = ar * si + ai * sr + b_i
            x_sc[pl.ds(r0, nb), :S5_SLAB_STATE] = nr.astype(BF16)
            x_sc[pl.ds(r0, nb), S5_SLAB_STATE:] = ni.astype(BF16)
            return nr, ni

        sr, si = lax.fori_loop(0, tl, body, (sr_ref[s], si_ref[s]))
        sr_ref[s] = sr
        si_ref[s] = si
        y_sc[:, sl] = y_sc[:, sl] + jnp.dot(x_sc[...], wc_ref[s], preferred_element_type=F32)
    yg = jax.nn.gelu(y_sc[...])
    z = jnp.dot(yg.astype(BF16), wglu_ref[...], preferred_element_type=F32) + bglu_ref[...]
    out = yg * jax.nn.sigmoid(z)
    y_ref[...] = out.astype(y_ref.dtype).reshape(tl, nb, W_B)


def _s5(u_tm, p, s0_re, s0_im, b, l):
    assert b % 16 == 0
    tl = min(64, l)
    assert l % tl == 0
    wb, wc, a_re, a_im = _s5_tables(p)
    to_slab = lambda s0: jnp.transpose(s0.astype(F32).reshape(b, S5_SLABS, S5_SLAB_STATE), (1, 0, 2))
    from_slab = lambda st: jnp.transpose(st, (1, 0, 2)).reshape(b, N_GROUPS, STATE_P)
    full = lambda shape: pl.BlockSpec(shape, lambda i: tuple(0 for _ in shape))
    st_shape = (S5_SLABS, b, S5_SLAB_STATE)
    rows = tl * b
    y, sr, si = pl.pallas_call(
        functools.partial(_s5_kernel, tl=tl, nb=b),
        grid=(l // tl,),
        in_specs=[pl.BlockSpec((tl, b, W_B), lambda i: (i, 0, 0)),
                  full(wb.shape), full(wc.shape), full(a_re.shape), full(a_im.shape),
                  full((1, W_B)), full((W_B, W_B)), full((1, W_B)), full(st_shape), full(st_shape)],
        out_specs=[pl.BlockSpec((tl, b, W_B), lambda i: (i, 0, 0)), full(st_shape), full(st_shape)],
        out_shape=[jax.ShapeDtypeStruct((l, b, W_B), BF16),
                   jax.ShapeDtypeStruct(st_shape, F32), jax.ShapeDtypeStruct(st_shape, F32)],
        scratch_shapes=[pltpu.VMEM((rows, 2 * S5_SLAB_STATE), F32),
                        pltpu.VMEM((rows, 2 * S5_SLAB_STATE), BF16),
                        pltpu.VMEM((rows, W_B), F32)],
        compiler_params=_cparams("arbitrary"),
        name="s5_scan",
    )(u_tm.reshape(l, b, W_B), wb, wc, a_re, a_im,
      p["d_skip"].astype(F32).reshape(1, W_B), p["w_glu"].astype(BF16), p["b_glu"].astype(F32).reshape(1, W_B),
      to_slab(s0_re), to_slab(s0_im))
    return y.reshape(l, b * W_B), from_slab(sr), from_slab(si)


def _merge_kernel(x_ref, oa_ref, yb_ref, om_ref, gt_ref, woa_ref, wob_ref, wom_ref, wout_ref, g2_ref,
                  xo_ref, hn_ref):
    pa = jnp.dot(oa_ref[...], woa_ref[...], preferred_element_type=F32)
    pb = jnp.dot(yb_ref[...], wob_ref[...], preferred_element_type=F32)
    pm = jnp.dot(om_ref[...], wom_ref[...], preferred_element_type=F32)
    d = D_MODEL
    mix = (gt_ref[:, :d].astype(F32) * pa + gt_ref[:, d:2 * d].astype(F32) * pb
           + gt_ref[:, 2 * d:].astype(F32) * pm)
    x = x_ref[...] + jnp.dot(mix.astype(BF16), wout_ref[...], preferred_element_type=F32)
    xo_ref[...] = x
    ms = jnp.mean(x * x, axis=-1, keepdims=True)
    hn_ref[...] = ((x * lax.rsqrt(ms + EPS)) * g2_ref[...]).astype(hn_ref.dtype)


def _merge(x, o_a, y_b_tm, o_m, gates, w_oa, w_ob, w_om, w_out, g2, b, l):
    t = b * l
    tm = min(512, l)
    assert l % tm == 0
    nl = l // tm
    row = lambda w: pl.BlockSpec((tm, w), lambda bi, i: (bi * nl + i, 0))
    full = lambda shape: pl.BlockSpec(shape, lambda bi, i: (0, 0))
    return pl.pallas_call(
        _merge_kernel,
        grid=(b, nl),
        in_specs=[row(D_MODEL), row(W_A), pl.BlockSpec((tm, W_B), lambda bi, i: (i, bi)), row(W_M), row(GATE_W),
                  full(w_oa.shape), full(w_ob.shape), full(w_om.shape), full(w_out.shape), full((1, D_MODEL))],
        out_specs=[row(D_MODEL), row(D_MODEL)],
        out_shape=[jax.ShapeDtypeStruct((t, D_MODEL), F32), jax.ShapeDtypeStruct((t, D_MODEL), BF16)],
        compiler_params=_cparams("parallel", "parallel"),
        name="merge",
    )(x, o_a, y_b_tm, o_m, gates, w_oa, w_ob, w_om, w_out, g2.reshape(1, D_MODEL).astype(F32))


def _ffn_kernel(x_ref, h_ref, wg_ref, wu_ref, wd_ref, o_ref, acc_sc):
    f = pl.program_id(1)

    @pl.when(f == 0)
    def _():
        acc_sc[...] = x_ref[...]

    h = h_ref[...]
    hg = jnp.dot(h, wg_ref[...], preferred_element_type=F32)
    hu = jnp.dot(h, wu_ref[...], preferred_element_type=F32)
    mid = (jax.nn.silu(hg) * hu).astype(BF16)
    acc_sc[...] += jnp.dot(mid, wd_ref[...], preferred_element_type=F32)

    @pl.when(f == pl.num_programs(1) - 1)
    def _():
        o_ref[...] = acc_sc[...]


def _ffn(x, hn, wg, wu, wd):
    t = x.shape[0]
    dff = wg.shape[1]
    tm = min(512, t)
    tf = dff // 2 if (dff // 2) % LANES == 0 else dff
    return pl.pallas_call(
        _ffn_kernel,
        grid=(t // tm, dff // tf),
        in_specs=[pl.BlockSpec((tm, D_MODEL), lambda i, f: (i, 0)), pl.BlockSpec((tm, D_MODEL), lambda i, f: (i, 0)),
                  pl.BlockSpec((D_MODEL, tf), lambda i, f: (0, f)), pl.BlockSpec((D_MODEL, tf), lambda i, f: (0, f)),
                  pl.BlockSpec((tf, D_MODEL), lambda i, f: (f, 0))],
        out_specs=pl.BlockSpec((tm, D_MODEL), lambda i, f: (i, 0)),
        out_shape=jax.ShapeDtypeStruct((t, D_MODEL), F32),
        scratch_shapes=[pltpu.VMEM((tm, D_MODEL), F32)],
        compiler_params=_cparams("parallel", "arbitrary"),
        name="ffn_dense",
    )(x, hn, wg, wu, wd)


def _router_kernel(h_ref, w_ref, tri_ref, meta_ref, cum_ref, tot_ref, carry_sc):
    i = pl.program_id(0)

    @pl.when(i == 0)
    def _():
        carry_sc[...] = jnp.zeros(carry_sc.shape, F32)

    lane = lax.broadcasted_iota(I32, (1, LANES), 1)
    logits = jnp.dot(h_ref[...], w_ref[...], preferred_element_type=F32)
    lg = jnp.where(lane < N_EXPERTS, logits, -jnp.inf)
    m1 = jnp.max(lg, axis=-1, keepdims=True)
    i1 = jnp.min(jnp.where(lg == m1, lane, LANES), axis=-1, keepdims=True)
    lg2 = jnp.where(lane == i1, -jnp.inf, lg)
    m2 = jnp.max(lg2, axis=-1, keepdims=True)
    i2 = jnp.min(jnp.where(lg2 == m2, lane, LANES), axis=-1, keepdims=True)
    e = jnp.exp(m2 - m1)
    g1 = 1.0 / (1.0 + e)
    g2 = e / (1.0 + e)
    hit1 = lane == i1
    hit2 = lane == i2
    cnt = jnp.where(hit1 | hit2, 1.0, 0.0)
    carry = carry_sc[...]
    before = jnp.dot(tri_ref[...], cnt.astype(BF16), preferred_element_type=F32) + carry
    r1 = jnp.sum(jnp.where(hit1, before, 0.0), axis=-1, keepdims=True)
    r2 = jnp.sum(jnp.where(hit2, before, 0.0), axis=-1, keepdims=True)
    cum_ref[0] = carry
    carry = carry + jnp.sum(cnt, axis=0, keepdims=True)
    carry_sc[...] = carry
    tot_ref[...] = carry
    meta = jnp.zeros(meta_ref.shape, F32)
    for c, val in enumerate((i1.astype(F32), i2.astype(F32), g1, g2, r1, r2)):
        meta = jnp.where(lane == c, val, meta)
    meta_ref[...] = meta


def _gather_kernel(ij_ref, is_ref, ifl_ref, h_ref, p0_ref, p1_ref, g0_ref, g1_ref, xs_ref, gs_ref, *, tg, ts):
    i = pl.program_id(0)
    fl = ifl_ref[i]

    @pl.when((fl & 2) != 0)
    def _():
        xs_ref[...] = jnp.zeros(xs_ref.shape, xs_ref.dtype)
        gs_ref[...] = jnp.zeros(gs_ref.shape, gs_ref.dtype)

    @pl.when((fl & 1) != 0)
    def _():
        rio = lax.broadcasted_iota(I32, (tg, ts), 0) + ij_ref[i] * tg
        m0 = p0_ref[0] == rio
        m1 = p1_ref[0] == rio
        sel = jnp.where(m0 | m1, 1.0, 0.0).astype(BF16)
        xs_ref[...] += jnp.dot(sel, h_ref[...], preferred_element_type=F32).astype(xs_ref.dtype)
        gs_ref[...] += jnp.sum(jnp.where(m0, g0_ref[0], 0.0) + jnp.where(m1, g1_ref[0], 0.0),
                               axis=-1, keepdims=True)


def _expert_kernel(be_ref, nv_ref, xs_ref, gs_ref, wg_ref, wu_ref, wd_ref, y_ref, acc_sc):
    j = pl.program_id(0)
    f = pl.program_id(1)
    valid = j < nv_ref[0]

    @pl.when(f == 0)
    def _():
        acc_sc[...] = jnp.zeros(acc_sc.shape, F32)

    @pl.when(valid)
    def _():
        x = xs_ref[...]
        hg = jnp.dot(x, wg_ref[...], preferred_element_type=F32)
        hu = jnp.dot(x, wu_ref[...], preferred_element_type=F32)
        mid = (jax.nn.silu(hg) * hu).astype(BF16)
        acc_sc[...] += jnp.dot(mid, wd_ref[...], preferred_element_type=F32)

    @pl.when(f == pl.num_programs(1) - 1)
    def _():
        y_ref[...] = jnp.where(valid, acc_sc[...] * gs_ref[...], 0.0).astype(y_ref.dtype)


def _combine_kernel(cj_ref, cs_ref, cfl_ref, x_ref, y_ref, p0_ref, p1_ref, o_ref, *, tg, ts):
    i = pl.program_id(0)
    fl = cfl_ref[i]

    @pl.when((fl & 2) != 0)
    def _():
        o_ref[...] = x_ref[...]

    @pl.when((fl & 1) != 0)
    def _():
        cio = lax.broadcasted_iota(I32, (ts, tg), 1) + cj_ref[i] * tg
        sel = jnp.where((p0_ref[...] == cio) | (p1_ref[...] == cio), 1.0, 0.0).astype(BF16)
        o_ref[...] += jnp.dot(sel, y_ref[...], preferred_element_type=F32)


def _moe(x, hn, router_w, wg, wu, wd):
    t = x.shape[0]
    dffe = wg.shape[2]
    ts = min(1024, t)
    tg = min(256, t)
    te = min(1024, t)
    tf = 512
    assert t % ts == 0 and te % tg == 0 and dffe % tf == 0
    nb = t // ts
    n_eblk = (2 * t) // te + N_EXPERTS
    ns = n_eblk * te
    n_gblk = ns // tg
    n_items = n_gblk + N_EXPERTS * nb

    w_pad = jnp.zeros((D_MODEL, LANES), BF16).at[:, :N_EXPERTS].set(router_w.astype(BF16))
    tri = (jnp.arange(ts)[:, None] > jnp.arange(ts)[None, :]).astype(BF16)
    meta, cum, tot = pl.pallas_call(
        _router_kernel,
        grid=(nb,),
        in_specs=[pl.BlockSpec((ts, D_MODEL), lambda i: (i, 0)), pl.BlockSpec((D_MODEL, LANES), lambda i: (0, 0)),
                  pl.BlockSpec((ts, ts), lambda i: (0, 0))],
        out_specs=[pl.BlockSpec((ts, LANES), lambda i: (i, 0)), pl.BlockSpec((1, 1, LANES), lambda i: (i, 0, 0)),
                   pl.BlockSpec((1, LANES), lambda i: (0, 0))],
        out_shape=[jax.ShapeDtypeStruct((t, LANES), F32), jax.ShapeDtypeStruct((nb, 1, LANES), F32),
                   jax.ShapeDtypeStruct((1, LANES), F32)],
        scratch_shapes=[pltpu.VMEM((1, LANES), F32)],
        compiler_params=_cparams("arbitrary"),
        name="moe_router",
    )(hn, w_pad, tri)

    e1 = meta[:, 0].astype(I32)
    e2 = meta[:, 1].astype(I32)
    g1 = meta[:, 2]
    g2 = meta[:, 3]
    r1 = meta[:, 4].astype(I32)
    r2 = meta[:, 5].astype(I32)
    counts = tot[0, :N_EXPERTS].astype(I32)
    cum_e = cum[:, 0, :N_EXPERTS].astype(I32)
    gpad = ((counts + te - 1) // te) * te
    gend = jnp.cumsum(gpad)
    gstart = gend - gpad
    pos0 = gstart[e1] + r1
    pos1 = gstart[e2] + r2
    nvalid_e = (gend[-1] // te).astype(I32)
    blk_e = jnp.arange(n_eblk, dtype=I32) * te
    be = jnp.minimum(jnp.sum(gend[None, :] <= blk_e[:, None], axis=1), N_EXPERTS - 1).astype(I32)
    be = jnp.where(jnp.arange(n_eblk) < nvalid_e, be, be[jnp.maximum(nvalid_e - 1, 0)])
    gb0 = jnp.arange(n_gblk, dtype=I32) * tg
    gb_valid = gb0 < gend[-1]
    gb_e = be[gb0 // te]
    rank0 = gb0 - gstart[gb_e]
    cnt_e = counts[gb_e]
    rank_last = jnp.maximum(jnp.minimum(rank0 + tg, cnt_e) - 1, 0)
    cum_cols = cum_e[:, gb_e]
    lo = jnp.sum(cum_cols <= jnp.minimum(rank0, rank_last)[None, :], axis=0) - 1
    hi = jnp.sum(cum_cols <= rank_last[None, :], axis=0) - 1
    n_it = jnp.where(gb_valid, hi - lo + 1, 0)
    it_end = jnp.cumsum(n_it)
    it_start = it_end - n_it
    total = it_end[-1]
    ii = jnp.arange(n_items, dtype=I32)
    live = ii < total
    iic = jnp.minimum(ii, total - 1)
    it_j = jnp.sum(it_end[None, :] <= iic[:, None], axis=1).astype(I32)
    it_s = (lo[it_j] + iic - it_start[it_j]).astype(I32)
    it_first = (iic == it_start[it_j]) & live
    it_fl = (live.astype(I32) + 2 * it_first.astype(I32)).astype(I32)
    order = jnp.argsort(jnp.where(live, it_s, nb), stable=True)
    cj = it_j[order]
    cs = it_s[order]
    cl = live[order]
    cs = jnp.where(cl, cs, cs[jnp.maximum(total - 1, 0)])
    cj = jnp.where(cl, cj, cj[jnp.maximum(total - 1, 0)])
    c_first = cl & jnp.concatenate([jnp.ones((1,), bool), cs[1:] != cs[:-1]])
    c_fl = (cl.astype(I32) + 2 * c_first.astype(I32)).astype(I32)

    row3 = lambda v: v.reshape(nb, 1, ts)
    col2 = lambda v: v.reshape(t, 1)

    gspec = pltpu.PrefetchScalarGridSpec(
        num_scalar_prefetch=3,
        grid=(n_items,),
        in_specs=[pl.BlockSpec((ts, D_MODEL), lambda i, ij, is_, fl: (is_[i], 0))]
        + [pl.BlockSpec((1, 1, ts), lambda i, ij, is_, fl: (is_[i], 0, 0)) for _ in range(4)],
        out_specs=[pl.BlockSpec((tg, D_MODEL), lambda i, ij, is_, fl: (ij[i], 0)),
                   pl.BlockSpec((tg, 1), lambda i, ij, is_, fl: (ij[i], 0))],
    )
    xs, gs = pl.pallas_call(
        functools.partial(_gather_kernel, tg=tg, ts=ts),
        grid_spec=gspec,
        out_shape=[jax.ShapeDtypeStruct((ns, D_MODEL), BF16), jax.ShapeDtypeStruct((ns, 1), F32)],
        compiler_params=_cparams("arbitrary"),
        name="moe_gather",
    )(it_j, it_s, it_fl, hn, row3(pos0), row3(pos1), row3(g1), row3(g2))

    espec = pltpu.PrefetchScalarGridSpec(
        num_scalar_prefetch=2,
        grid=(n_eblk, dffe // tf),
        in_specs=[pl.BlockSpec((te, D_MODEL), lambda j, f, be_, nv: (j, 0)),
                  pl.BlockSpec((te, 1), lambda j, f, be_, nv: (j, 0)),
                  pl.BlockSpec((None, D_MODEL, tf), lambda j, f, be_, nv: (be_[j], 0, f)),
                  pl.BlockSpec((None, D_MODEL, tf), lambda j, f, be_, nv: (be_[j], 0, f)),
                  pl.BlockSpec((None, tf, D_MODEL), lambda j, f, be_, nv: (be_[j], f, 0))],
        out_specs=pl.BlockSpec((te, D_MODEL), lambda j, f, be_, nv: (j, 0)),
        scratch_shapes=[pltpu.VMEM((te, D_MODEL), F32)],
    )
    y = pl.pallas_call(
        _expert_kernel,
        grid_spec=espec,
        out_shape=jax.ShapeDtypeStruct((ns, D_MODEL), BF16),
        compiler_params=_cparams("parallel", "arbitrary"),
        name="moe_experts",
    )(be, nvalid_e.reshape(1), xs, gs, wg, wu, wd)

    cspec = pltpu.PrefetchScalarGridSpec(
        num_scalar_prefetch=3,
        grid=(n_items,),
        in_specs=[pl.BlockSpec((ts, D_MODEL), lambda i, cj_, cs_, fl: (cs_[i], 0)),
                  pl.BlockSpec((tg, D_MODEL), lambda i, cj_, cs_, fl: (cj_[i], 0)),
                  pl.BlockSpec((ts, 1), lambda i, cj_, cs_, fl: (cs_[i], 0)),
                  pl.BlockSpec((ts, 1), lambda i, cj_, cs_, fl: (cs_[i], 0))],
        out_specs=pl.BlockSpec((ts, D_MODEL), lambda i, cj_, cs_, fl: (cs_[i], 0)),
    )
    return pl.pallas_call(
        functools.partial(_combine_kernel, tg=tg, ts=ts),
        grid_spec=cspec,
        out_shape=jax.ShapeDtypeStruct((t, D_MODEL), F32),
        compiler_params=_cparams("arbitrary"),
        name="moe_combine",
    )(cj, cs, c_fl, x, y, col2(pos0), col2(pos1))


def _layer(x, b, l, li, p, mem_k, mem_v, past):
    h = _rms(x, p["norm1"])
    w_in = p["w_in"]
    seg = lambda a: w_in[:, SPLITS[a]:(SPLITS[a + 1] if a + 1 < len(SPLITS) else None)]
    tile = lambda g, n: jnp.tile(g.astype(F32), n)
    (q,) = _mm(h, seg(0), epi="seg64", gain=tile(p["q_norm_a"], 2 * H_A) * (DH_A ** -0.5), tn=512)
    k32, kbf = _mm(h, seg(1), epi="seg64", gain=tile(p["k_norm_a"], 2 * H_A), out_dtypes=(F32, BF16), tn=512)
    v32, vbf = _mm(h, seg(2), out_dtypes=(F32, BF16), tn=512)
    (u_tm,) = _mm(h, seg(3), out_dtypes=(F32,), time_major=(b, l))
    (qm,) = _mm(h, seg(4), epi="seg192", gain=tile(p["q_norm_m"], H_M) * (DH_M ** -0.5))
    (gates,) = _mm(h, seg(5), epi="sigmoid", tn=512)

    lam_init = 0.8 - 0.6 * math.exp(-0.3 * li)
    lam = (jnp.exp(jnp.sum(p["lam_q1"].astype(F32) * p["lam_k1"].astype(F32)))
           - jnp.exp(jnp.sum(p["lam_q2"].astype(F32) * p["lam_k2"].astype(F32))) + lam_init)
    if past is None:
        o_a = _attn_prompt(q, kbf, vbf, p["rel_bias"], p["subln_a"], lam, lam_init, b, l)
        zero = jnp.zeros((b, N_GROUPS, STATE_P), F32)
        y_b, s_re, s_im = _s5(u_tm, p, zero, zero, b, l)
    else:
        k_past, v_past, s0_re, s0_im = past
        o_a = _attn_sample(q, kbf, vbf, k_past, v_past, p["rel_bias"], p["subln_a"], lam, lam_init, b, l)
        y_b, s_re, s_im = _s5(u_tm, p, s0_re, s0_im, b, l)
    o_m = _memattn(qm, mem_k, mem_v, b, l)
    x, hn = _merge(x, o_a, y_b, o_m, gates, p["w_oa"], p["w_ob"], p["w_om"], p["w_out"], p["norm2"], b, l)
    if li % 2 == 0:
        x = _ffn(x, hn, p["ffn_w_gate"], p["ffn_w_up"], p["ffn_w_down"])
    else:
        x = _moe(x, hn, p["router"], p["moe_w_gate"], p["moe_w_up"], p["moe_w_down"])
    return x, k32, v32, s_re, s_im


def kernel(x_prompt, x_sample, mem_prompt, cache_attn_k, cache_attn_v, state_ssm_re, state_ssm_im, cache_mem_k, cache_mem_v, norm1, norm2, w_in, q_norm_a, k_norm_a, lam_q1, lam_k1, lam_q2, lam_k2, subln_a, w_oa, lambda_re, lambda_im, log_dt, b_re, b_im, c_re, c_im, d_skip, w_glu, b_glu, w_ob, w_mk, w_mv, q_norm_m, k_norm_m, w_om, w_out, rel_bias, ffn_w_gate, ffn_w_up, ffn_w_down, router, moe_w_gate, moe_w_up, moe_w_down):
    depth = w_in.shape[0]
    bp, lp, _ = x_prompt.shape
    bs, ls, _ = x_sample.shape
    n_mem = mem_prompt.shape[1]
    xp = x_prompt.reshape(bp * lp, D_MODEL)
    xs = x_sample.reshape(bs * ls, D_MODEL)
    mem_bf = mem_prompt.reshape(bp * n_mem, D_MODEL).astype(BF16)
    outs = {name: [] for name in ("kp", "vp", "srp", "sip", "mkp", "mvp", "ks", "vs", "srs", "sis")}
    for li in range(depth):
        j = li // 2
        p = {
            "norm1": norm1[li], "norm2": norm2[li], "w_in": w_in[li].astype(BF16),
            "q_norm_a": q_norm_a[li], "k_norm_a": k_norm_a[li],
            "lam_q1": lam_q1[li], "lam_k1": lam_k1[li], "lam_q2": lam_q2[li], "lam_k2": lam_k2[li],
            "subln_a": subln_a[li], "w_oa": w_oa[li].astype(BF16),
            "lambda_re": lambda_re[li], "lambda_im": lambda_im[li], "log_dt": log_dt[li],
            "b_re": b_re[li], "b_im": b_im[li], "c_re": c_re[li], "c_im": c_im[li],
            "d_skip": d_skip[li], "w_glu": w_glu[li], "b_glu": b_glu[li], "w_ob": w_ob[li].astype(BF16),
            "q_norm_m": q_norm_m[li], "w_om": w_om[li].astype(BF16), "w_out": w_out[li].astype(BF16),
            "rel_bias": rel_bias,
        }
        if li % 2 == 0:
            p.update(ffn_w_gate=ffn_w_gate[j].astype(BF16), ffn_w_up=ffn_w_up[j].astype(BF16),
                     ffn_w_down=ffn_w_down[j].astype(BF16))
        else:
            p.update(router=router[j], moe_w_gate=moe_w_gate[j].astype(BF16), moe_w_up=moe_w_up[j].astype(BF16),
                     moe_w_down=moe_w_down[j].astype(BF16))
        mk32, mkbf = _mm(mem_bf, w_mk[li].astype(BF16), epi="seg192", gain=jnp.tile(k_norm_m[li].astype(F32), H_M),
                         out_dtypes=(F32, BF16))
        mv32, mvbf = _mm(mem_bf, w_mv[li].astype(BF16), out_dtypes=(F32, BF16))
        xp, k_p, v_p, sr_p, si_p = _layer(xp, bp, lp, li, p, mkbf.reshape(bp, n_mem, W_M),
                                          mvbf.reshape(bp, n_mem, W_M), None)
        past = (cache_attn_k[li], cache_attn_v[li], state_ssm_re[li], state_ssm_im[li])
        xs, k_s, v_s, sr_s, si_s = _layer(xs, bs, ls, li, p,
                                          cache_mem_k[li].reshape(bs, n_mem, W_M).astype(BF16),
                                          cache_mem_v[li].reshape(bs, n_mem, W_M).astype(BF16), past)
        outs["kp"].append(k_p.reshape(bp, lp, H_A, DV_A))
        outs["vp"].append(v_p.reshape(bp, lp, H_A, DV_A))
        outs["srp"].append(sr_p)
        outs["sip"].append(si_p)
        outs["mkp"].append(mk32.reshape(bp, n_mem, H_M, DH_M))
        outs["mvp"].append(mv32.reshape(bp, n_mem, H_M, DH_M))
        outs["ks"].append(k_s.reshape(bs, ls, H_A, DV_A))
        outs["vs"].append(v_s.reshape(bs, ls, H_A, DV_A))
        outs["srs"].append(sr_s)
        outs["sis"].append(si_s)
    st = lambda name: jnp.stack(outs[name])
    return (xp.reshape(bp, lp, D_MODEL), xs.reshape(bs, ls, D_MODEL), st("kp"), st("vp"), st("srp"), st("sip"),
            st("mkp"), st("mvp"), st("ks"), st("vs"), st("srs"), st("sis"))
```

```python
import functools
import math

import jax
import jax.numpy as jnp
from jax import lax
from jax.experimental import pallas as pl
from jax.experimental.pallas import tpu as pltpu

F32 = jnp.float32
BF16 = jnp.bfloat16
I32 = jnp.int32

D_MODEL = 1024
CHUNK = 64
H_A = 8
DH_A = 64
DV_A = 2 * DH_A
W_A = H_A * DV_A
GROUP_CH = 16
N_GROUPS = 48
STATE_P = 64
W_B = N_GROUPS * GROUP_CH
H_M = 4
DH_M = 192
W_M = H_M * DH_M
REL_BUCKETS = 32
REL_MAX_DIST = 128
N_EXPERTS = 8
EPS = 1e-6
NEG_INF = -1e30
LOG2E = math.log2(math.e)
Q_A_W = H_A * 2 * DH_A
SPLITS = (0, Q_A_W, 2 * Q_A_W, 2 * Q_A_W + W_A, 2 * Q_A_W + W_A + W_B, 2 * Q_A_W + W_A + W_B + W_M)
GATE_W = 3 * D_MODEL

LANES = 128
VMEM_LIMIT = 56 * 1024 * 1024
S5_SLAB_GROUPS = LANES // GROUP_CH
S5_SLABS = N_GROUPS // S5_SLAB_GROUPS
S5_SLAB_STATE = S5_SLAB_GROUPS * STATE_P


def _cparams(*sem):
    return pltpu.CompilerParams(dimension_semantics=sem, vmem_limit_bytes=VMEM_LIMIT)


def _rms_kernel(x_ref, g_ref, o_ref):
    x = x_ref[...]
    ms = jnp.mean(x * x, axis=-1, keepdims=True)
    o_ref[...] = ((x * lax.rsqrt(ms + EPS)) * g_ref[...]).astype(o_ref.dtype)


def _rms(x, g):
    t, d = x.shape
    tm = min(1024, t)
    return pl.pallas_call(
        _rms_kernel,
        grid=(t // tm,),
        in_specs=[pl.BlockSpec((tm, d), lambda i: (i, 0)), pl.BlockSpec((1, d), lambda i: (0, 0))],
        out_specs=pl.BlockSpec((tm, d), lambda i: (i, 0)),
        out_shape=jax.ShapeDtypeStruct((t, d), BF16),
        compiler_params=_cparams("parallel"),
        name="rmsnorm",
    )(x, g.reshape(1, d).astype(F32))


def _seg64_scale(ys):
    lane = lax.broadcasted_iota(I32, (1, LANES), 1)
    left = lane < DH_A
    y2 = ys * ys
    sl = jnp.sum(jnp.where(left, y2, 0.0), axis=-1, keepdims=True)
    sr = jnp.sum(jnp.where(left, 0.0, y2), axis=-1, keepdims=True)
    rl = lax.rsqrt(sl * (1.0 / DH_A) + EPS)
    rr = lax.rsqrt(sr * (1.0 / DH_A) + EPS)
    return jnp.where(left, rl, rr)


def _seg192_scale(y):
    col = lax.broadcasted_iota(I32, (1, W_M), 1)
    y2 = y * y
    rb = jnp.zeros_like(y)
    for h in range(H_M):
        m = (col >= DH_M * h) & (col < DH_M * (h + 1))
        s = jnp.sum(jnp.where(m, y2, 0.0), axis=-1, keepdims=True)
        rb = jnp.where(m, lax.rsqrt(s / DH_M + EPS), rb)
    return rb


def _mm_kernel(*refs, epi, layouts, has_gain):
    a_ref, w_ref = refs[0], refs[1]
    g_ref = refs[2] if has_gain else None
    outs = refs[2 + int(has_gain):]
    assert len(outs) == len(layouts)
    y = jnp.dot(a_ref[...], w_ref[...], preferred_element_type=F32)
    tn = y.shape[1]
    if epi == "seg192":
        y = (y * _seg192_scale(y)) * g_ref[...]
    elif epi == "sigmoid":
        y = jax.nn.sigmoid(y)
    if epi == "seg64" or "head" in layouts:
        for s in range(tn // LANES):
            sl = slice(s * LANES, (s + 1) * LANES)
            ys = y[:, sl]
            if epi == "seg64":
                ys = (ys * _seg64_scale(ys)) * g_ref[:, sl]
            for o, lay in zip(outs, layouts):
                if lay == "head":
                    o[s] = ys.astype(o.dtype)
                else:
                    o[:, sl] = ys.astype(o.dtype)
        return
    for o in outs:
        o[...] = y.astype(o.dtype)


def _mm(a, w, *, epi="plain", gain=None, out_dtypes=(BF16,), layouts=None, tn=None, bl=None):
    t, k = a.shape
    n = w.shape[1]
    tn = n if tn is None else tn
    layouts = ("tok",) * len(out_dtypes) if layouts is None else layouts
    assert n % tn == 0 and tn % LANES == 0
    if bl is None:
        assert all(lay == "tok" for lay in layouts)
        tm, nl = min(1024, t), 1
    else:
        b, l = bl
        tm = min(1024, l)
        assert l % tm == 0
        nl = l // tm
    assert t % tm == 0
    out_specs, out_shapes = [], []
    for dt, lay in zip(out_dtypes, layouts):
        if lay == "tok":
            out_specs.append(pl.BlockSpec((tm, tn), lambda i, j: (i, j)))
            out_shapes.append(jax.ShapeDtypeStruct((t, n), dt))
        elif lay == "time":
            assert tn == n
            out_specs.append(pl.BlockSpec((tm, tn), lambda i, j: (i % nl, i // nl)))
            out_shapes.append(jax.ShapeDtypeStruct((l, b * n), dt))
        else:
            out_specs.append(pl.BlockSpec((None, tn // LANES, tm, LANES), lambda i, j: (i // nl, j, i % nl, 0)))
            out_shapes.append(jax.ShapeDtypeStruct((b, n // LANES, l, LANES), dt))
    in_specs = [pl.BlockSpec((tm, k), lambda i, j: (i, 0)), pl.BlockSpec((k, tn), lambda i, j: (0, j))]
    args = [a, w]
    if gain is not None:
        in_specs.append(pl.BlockSpec((1, tn), lambda i, j: (0, j)))
        args.append(gain.reshape(1, n).astype(F32))
    return pl.pallas_call(
        functools.partial(_mm_kernel, epi=epi, layouts=tuple(layouts), has_gain=gain is not None),
        grid=(t // tm, n // tn),
        in_specs=in_specs,
        out_specs=out_specs,
        out_shape=out_shapes,
        compiler_params=_cparams("parallel", "parallel"),
        name="mm_" + epi,
    )(*args)


def _rel_bias(q_pos, k_pos, table):
    rel = k_pos[None, :] - q_pos[:, None]
    half = REL_BUCKETS // 2
    max_exact = half // 2
    n = jnp.abs(rel)
    nf = jnp.maximum(n, 1).astype(F32)
    large = max_exact + (jnp.log(nf / max_exact) / math.log(REL_MAX_DIST / max_exact)
                         * (half - max_exact)).astype(I32)
    large = jnp.minimum(large, half - 1)
    bucket = jnp.where(rel > 0, half, 0) + jnp.where(n < max_exact, n, large)
    bias = jnp.transpose(table[bucket].astype(F32), (2, 0, 1))
    visible = (k_pos[None, :] // CHUNK) <= (q_pos[:, None] // CHUNK)
    return jnp.where(visible[None], bias, NEG_INF)


def _split_halves(q):
    lane = lax.broadcasted_iota(I32, (1, LANES), 1)
    zero = jnp.zeros_like(q)
    return jnp.concatenate([jnp.where(lane < DH_A, q, zero), jnp.where(lane >= DH_A, q, zero)], axis=0)


def _subln(o, g, lam_init):
    ms = jnp.mean(o * o, axis=-1, keepdims=True)
    return ((o * lax.rsqrt(ms + EPS)) * g) * (1.0 - lam_init)


def _attn_prompt_kernel(lam_ref, q_ref, k_ref, v_ref, bias_ref, g_ref, o_ref,
                        s_sc, m_sc, l_sc, acc_sc, *, tq, lam_init):
    qi = pl.program_id(2)
    n_blk = qi + 1
    qs = _split_halves(q_ref[...])
    nt = (((1,), (1,)), ((), ()))
    nc = tq // LANES

    def fold(op, s):
        r = s[:, :LANES]
        for c in range(1, nc):
            r = op(r, s[:, c * LANES:(c + 1) * LANES])
        return r

    def score(j):
        r0 = pl.multiple_of(j * tq, tq)
        bias = bias_ref[jnp.clip(j - (qi - 2), 0, 2)]
        s = lax.dot_general(qs, k_ref[pl.ds(r0, tq), :], nt, preferred_element_type=F32)
        s = s + jnp.concatenate([bias, bias], axis=0)
        s_sc[j] = s
        return fold(jnp.maximum, s)

    m_sc[...] = jnp.full(m_sc.shape, -jnp.inf, F32)

    def score_pair(g, c):
        m_sc[...] = jnp.maximum(m_sc[...], jnp.maximum(score(2 * g), score(2 * g + 1)))
        return c

    lax.fori_loop(0, n_blk // 2, score_pair, 0)

    @pl.when(n_blk % 2 == 1)
    def _():
        m_sc[...] = jnp.maximum(m_sc[...], score(qi))

    m = jnp.max(m_sc[...], axis=-1, keepdims=True)
    m_sc[...] = jnp.broadcast_to(m, m_sc.shape)
    l_sc[...] = jnp.zeros(l_sc.shape, F32)
    acc_sc[...] = jnp.zeros(acc_sc.shape, F32)

    def probs(j):
        mrow = m_sc[...]
        s = s_sc[j]
        p = jnp.concatenate([jnp.exp2(s[:, cc * LANES:(cc + 1) * LANES] - mrow) for cc in range(nc)], axis=1)
        r0 = pl.multiple_of(j * tq, tq)
        return fold(jnp.add, p), jnp.dot(p.astype(BF16), v_ref[pl.ds(r0, tq), :], preferred_element_type=F32)

    def pv_pair(g, c):
        l0, a0 = probs(2 * g)
        l1, a1 = probs(2 * g + 1)
        l_sc[...] += l0 + l1
        acc_sc[...] += a0 + a1
        return c

    lax.fori_loop(0, n_blk // 2, pv_pair, 0)

    @pl.when(n_blk % 2 == 1)
    def _():
        l0, a0 = probs(qi)
        l_sc[...] += l0
        acc_sc[...] += a0

    o = acc_sc[...] / jnp.sum(l_sc[...], axis=-1, keepdims=True)
    o = o[:tq] - lam_ref[0] * o[tq:]
    o_ref[...] = _subln(o, g_ref[...], lam_init).astype(o_ref.dtype)


def _attn_prompt(q, k, v, rel_table, subln_g, lam, lam_init, b, l):
    tq = min(512, l)
    assert l % tq == 0 and tq % CHUNK == 0 and tq >= 128
    cfar = _rel_bias(jnp.array([2 * tq + 1]), jnp.array([0]), rel_table)[:, 0, 0]
    bias = _rel_bias(jnp.arange(tq, 2 * tq), jnp.arange(2 * tq), rel_table)
    bias = (bias - cfar[:, None, None]) * LOG2E
    bias = jnp.stack([jnp.zeros_like(bias[:, :, :tq]), bias[:, :, :tq], bias[:, :, tq:]], axis=1)
    smem = pl.BlockSpec(memory_space=pltpu.SMEM)
    out = pl.pallas_call(
        functools.partial(_attn_prompt_kernel, tq=tq, lam_init=lam_init),
        grid=(b, H_A, l // tq),
        in_specs=[smem,
                  pl.BlockSpec((None, None, tq, DV_A), lambda bi, h, i: (bi, h, i, 0)),
                  pl.BlockSpec((None, None, l, DV_A), lambda bi, h, i: (bi, h, 0, 0)),
                  pl.BlockSpec((None, None, l, DV_A), lambda bi, h, i: (bi, h, 0, 0)),
                  pl.BlockSpec((None, 3, tq, tq), lambda bi, h, i: (h, 0, 0, 0)),
                  pl.BlockSpec((1, DV_A), lambda bi, h, i: (0, 0))],
        out_specs=pl.BlockSpec((None, tq, DV_A), lambda bi, h, i: (bi, i, h)),
        out_shape=jax.ShapeDtypeStruct((b, l, W_A), BF16),
        scratch_shapes=[pltpu.VMEM((l // tq, 2 * tq, tq), F32), pltpu.VMEM((2 * tq, LANES), F32),
                        pltpu.VMEM((2 * tq, LANES), F32), pltpu.VMEM((2 * tq, DV_A), F32)],
        compiler_params=_cparams("parallel", "parallel", "arbitrary"),
        name="attn_prompt",
    )(lam.reshape(1), q, k, v, bias, subln_g.reshape(1, DV_A).astype(F32))
    return out.reshape(b * l, W_A)


def _attn_sample_kernel(lam_ref, q_ref, kp_ref, vp_ref, kn_ref, vn_ref, bp_ref, bn_ref, g_ref, o_ref,
                        *, lq, lam_init):
    lam = lam_ref[0]
    nt = (((1,), (1,)), ((), ()))
    for h in range(H_A):
        sl = slice(h * DV_A, (h + 1) * DV_A)
        qs = _split_halves(q_ref[h])
        kp = kp_ref[:, sl].astype(BF16)
        vp = vp_ref[:, sl].astype(BF16)
        bp = bp_ref[h]
        bn = bn_ref[h]
        sp = lax.dot_general(qs, kp, nt, preferred_element_type=F32) + jnp.concatenate([bp, bp], axis=0)
        sn = lax.dot_general(qs, kn_ref[h], nt, preferred_element_type=F32) + jnp.concatenate([bn, bn], axis=0)
        m = jnp.maximum(jnp.max(sp, axis=-1, keepdims=True), jnp.max(sn, axis=-1, keepdims=True))
        pp = jnp.exp2(sp - m)
        pn = jnp.exp2(sn - m)
        lsum = jnp.sum(pp, axis=-1, keepdims=True) + jnp.sum(pn, axis=-1, keepdims=True)
        o = (jnp.dot(pp.astype(BF16), vp, preferred_element_type=F32)
             + jnp.dot(pn.astype(BF16), vn_ref[h], preferred_element_type=F32)) / lsum
        o = o[:lq] - lam * o[lq:]
        o_ref[:, sl] = _subln(o, g_ref[...], lam_init).astype(o_ref.dtype)


def _attn_sample(q, k_new, v_new, k_past, v_past, rel_table, subln_g, lam, lam_init, b, l):
    past = k_past.shape[1]
    k_pos = jnp.arange(past + l)
    bias = _rel_bias(k_pos[past:], k_pos, rel_table) * LOG2E
    bp, bn = bias[:, :, :past], bias[:, :, past:]
    smem = pl.BlockSpec(memory_space=pltpu.SMEM)
    tok = lambda rows: pl.BlockSpec((None, rows, W_A), lambda bi: (bi, 0, 0))
    head = pl.BlockSpec((None, H_A, l, DV_A), lambda bi: (bi, 0, 0, 0))
    full = lambda shape: pl.BlockSpec(shape, lambda bi: tuple(0 for _ in shape))
    out = pl.pallas_call(
        functools.partial(_attn_sample_kernel, lq=l, lam_init=lam_init),
        grid=(b,),
        in_specs=[smem, head, tok(past), tok(past), head, head,
                  full((H_A, l, past)), full((H_A, l, l)), full((1, DV_A))],
        out_specs=tok(l),
        out_shape=jax.ShapeDtypeStruct((b, l, W_A), BF16),
        compiler_params=_cparams("parallel"),
        name="attn_sample",
    )(lam.reshape(1), q, k_past.reshape(b, past, W_A), v_past.reshape(b, past, W_A),
      k_new, v_new, bp, bn, subln_g.reshape(1, DV_A).astype(F32))
    return out.reshape(b * l, W_A)


def _memattn_kernel(q_ref, mk_ref, mv_ref, o_ref, *, tq):
    col = lax.broadcasted_iota(I32, (1, W_M), 1)
    q = q_ref[...]
    zero = jnp.zeros_like(q)
    masks = [(col >= DH_M * h) & (col < DH_M * (h + 1)) for h in range(H_M)]
    qs = jnp.concatenate([jnp.where(m, q, zero) for m in masks], axis=0)
    s = lax.dot_general(qs, mk_ref[...], (((1,), (1,)), ((), ())), preferred_element_type=F32)
    m = jnp.max(s, axis=-1, keepdims=True)
    p = jnp.exp(s - m)
    p = p / jnp.sum(p, axis=-1, keepdims=True)
    o_all = jnp.dot(p.astype(BF16), mv_ref[...], preferred_element_type=F32)
    o = jnp.zeros((tq, W_M), F32)
    for h in range(H_M):
        o = jnp.where(masks[h], o_all[h * tq:(h + 1) * tq], o)
    o_ref[...] = o.astype(o_ref.dtype)


def _memattn(qm, mk, mv, b, l):
    n_mem = mk.shape[1]
    tq = min(256, l)
    assert l % tq == 0
    out = pl.pallas_call(
        functools.partial(_memattn_kernel, tq=tq),
        grid=(b, l // tq),
        in_specs=[pl.BlockSpec((None, tq, W_M), lambda bi, i: (bi, i, 0)),
                  pl.BlockSpec((None, n_mem, W_M), lambda bi, i: (bi, 0, 0)),
                  pl.BlockSpec((None, n_mem, W_M), lambda bi, i: (bi, 0, 0))],
        out_specs=pl.BlockSpec((None, tq, W_M), lambda bi, i: (bi, i, 0)),
        out_shape=jax.ShapeDtypeStruct((b, l, W_M), BF16),
        compiler_params=_cparams("parallel", "parallel"),
        name="memattn",
    )(qm.reshape(b, l, W_M), mk, mv)
    return out.reshape(b * l, W_M)


def _s5_tables(p):
    dt = jnp.exp(p["log_dt"].astype(F32))[:, None]
    lr = jnp.minimum(p["lambda_re"].astype(F32), -1e-4)
    lim = p["lambda_im"].astype(F32)
    mag = jnp.exp(lr * dt)
    ar = mag * jnp.cos(lim * dt)
    ai = mag * jnp.sin(lim * dt)
    den = lr * lr + lim * lim
    fr = ((ar - 1.0) * lr + ai * lim) / den
    fi = (ai * lr - (ar - 1.0) * lim) / den
    br = p["b_re"].astype(F32)
    bi = p["b_im"].astype(F32)
    bbr = fr[..., None] * br - fi[..., None] * bi
    bbi = fr[..., None] * bi + fi[..., None] * br
    eye = jnp.eye(S5_SLAB_GROUPS, dtype=F32)
    sg = (S5_SLABS, S5_SLAB_GROUPS)

    def in_w(bb):
        w = jnp.einsum("sgpc,gh->sgchp", bb.reshape(*sg, STATE_P, GROUP_CH), eye)
        return w.reshape(S5_SLABS, LANES, S5_SLAB_STATE)

    def out_w(c):
        w = jnp.einsum("sgcp,gh->sgphc", c.reshape(*sg, GROUP_CH, STATE_P), eye)
        return w.reshape(S5_SLABS, S5_SLAB_STATE, LANES)

    wb = jnp.concatenate([in_w(bbr), in_w(bbi)], axis=2).astype(BF16)
    wc = jnp.concatenate([out_w(p["c_re"].astype(F32)), -out_w(p["c_im"].astype(F32))], axis=1).astype(BF16)
    a_re = ar.reshape(S5_SLABS, 1, S5_SLAB_STATE)
    a_im = ai.reshape(S5_SLABS, 1, S5_SLAB_STATE)
    return wb, wc, a_re, a_im


def _s5_kernel(u_ref, wb_ref, wc_ref, ar_ref, ai_ref, d_ref, wglu_ref, bglu_ref, s0r_ref, s0i_ref,
               y_ref, sr_ref, si_ref, bu_sc, x_sc, y_sc, *, tl, nb):
    i = pl.program_id(0)

    @pl.when(i == 0)
    def _():
        sr_ref[...] = s0r_ref[...]
        si_ref[...] = s0i_ref[...]

    rows = tl * nb
    u = u_ref[...].reshape(rows, W_B)
    y_sc[...] = d_ref[...] * u
    for s in range(S5_SLABS):
        sl = slice(s * LANES, (s + 1) * LANES)
        bu_sc[...] = jnp.dot(u[:, sl].astype(BF16), wb_ref[s], preferred_element_type=F32)
        ar = jnp.broadcast_to(ar_ref[s], (nb, S5_SLAB_STATE))
        ai = jnp.broadcast_to(ai_ref[s], (nb, S5_SLAB_STATE))

        def body(t, carry):
            sr, si = carry
            r0 = pl.multiple_of(t * nb, nb)
            b_r = bu_sc[pl.ds(r0, nb), :S5_SLAB_STATE]
            b_i = bu_sc[pl.ds(r0, nb), S5_SLAB_STATE:]
            nr = ar * sr - ai * si + b_r
            ni = ar * si + ai * sr + b_i
            x_sc[pl.ds(r0, nb), :S5_SLAB_STATE] = nr.astype(BF16)
            x_sc[pl.ds(r0, nb), S5_SLAB_STATE:] = ni.astype(BF16)
            return nr, ni

        sr, si = lax.fori_loop(0, tl, body, (sr_ref[s], si_ref[s]))
        sr_ref[s] = sr
        si_ref[s] = si
        y_sc[:, sl] = y_sc[:, sl] + jnp.dot(x_sc[...], wc_ref[s], preferred_element_type=F32)
    yg = jax.nn.gelu(y_sc[...])
    z = jnp.dot(yg.astype(BF16), wglu_ref[...], preferred_element_type=F32) + bglu_ref[...]
    out = yg * jax.nn.sigmoid(z)
    y_ref[...] = out.astype(y_ref.dtype).reshape(tl, nb, W_B)


def _s5(u_tm, p, s0_re, s0_im, b, l):
    assert b % 16 == 0
    tl = min(64, l)
    assert l % tl == 0
    wb, wc, a_re, a_im = _s5_tables(p)
    to_slab = lambda s0: jnp.transpose(s0.astype(F32).reshape(b, S5_SLABS, S5_SLAB_STATE), (1, 0, 2))
    from_slab = lambda st: jnp.transpose(st, (1, 0, 2)).reshape(b, N_GROUPS, STATE_P)
    full = lambda shape: pl.BlockSpec(shape, lambda i: tuple(0 for _ in shape))
    st_shape = (S5_SLABS, b, S5_SLAB_STATE)
    rows = tl * b
    y, sr, si = pl.pallas_call(
        functools.partial(_s5_kernel, tl=tl, nb=b),
        grid=(l // tl,),
        in_specs=[pl.BlockSpec((tl, b, W_B), lambda i: (i, 0, 0)),
                  full(wb.shape), full(wc.shape), full(a_re.shape), full(a_im.shape),
                  full((1, W_B)), full((W_B, W_B)), full((1, W_B)), full(st_shape), full(st_shape)],
        out_specs=[pl.BlockSpec((tl, b, W_B), lambda i: (i, 0, 0)), full(st_shape), full(st_shape)],
        out_shape=[jax.ShapeDtypeStruct((l, b, W_B), BF16),
                   jax.ShapeDtypeStruct(st_shape, F32), jax.ShapeDtypeStruct(st_shape, F32)],
        scratch_shapes=[pltpu.VMEM((rows, 2 * S5_SLAB_STATE), F32),
                        pltpu.VMEM((rows, 2 * S5_SLAB_STATE), BF16),
                        pltpu.VMEM((rows, W_B), F32)],
        compiler_params=_cparams("arbitrary"),
        name="s5_scan",
    )(u_tm.reshape(l, b, W_B), wb, wc, a_re, a_im,
      p["d_skip"].astype(F32).reshape(1, W_B), p["w_glu"].astype(BF16), p["b_glu"].astype(F32).reshape(1, W_B),
      to_slab(s0_re), to_slab(s0_im))
    return y.reshape(l, b * W_B), from_slab(sr), from_slab(si)


def _merge_kernel(x_ref, oa_ref, yb_ref, om_ref, gt_ref, woa_ref, wob_ref, wom_ref, wout_ref, g2_ref,
                  xo_ref, hn_ref):
    pa = jnp.dot(oa_ref[...], woa_ref[...], preferred_element_type=F32)
    pb = jnp.dot(yb_ref[...], wob_ref[...], preferred_element_type=F32)
    pm = jnp.dot(om_ref[...], wom_ref[...], preferred_element_type=F32)
    d = D_MODEL
    mix = (gt_ref[:, :d].astype(F32) * pa + gt_ref[:, d:2 * d].astype(F32) * pb
           + gt_ref[:, 2 * d:].astype(F32) * pm)
    x = x_ref[...] + jnp.dot(mix.astype(BF16), wout_ref[...], preferred_element_type=F32)
    xo_ref[...] = x
    ms = jnp.mean(x * x, axis=-1, keepdims=True)
    hn_ref[...] = ((x * lax.rsqrt(ms + EPS)) * g2_ref[...]).astype(hn_ref.dtype)


def _merge(x, o_a, y_b_tm, o_m, gates, w_oa, w_ob, w_om, w_out, g2, b, l):
    t = b * l
    tm = min(512, l)
    assert l % tm == 0
    nl = l // tm
    row = lambda w: pl.BlockSpec((tm, w), lambda bi, i: (bi * nl + i, 0))
    full = lambda shape: pl.BlockSpec(shape, lambda bi, i: (0, 0))
    return pl.pallas_call(
        _merge_kernel,
        grid=(b, nl),
        in_specs=[row(D_MODEL), row(W_A), pl.BlockSpec((tm, W_B), lambda bi, i: (i, bi)), row(W_M), row(GATE_W),
                  full(w_oa.shape), full(w_ob.shape), full(w_om.shape), full(w_out.shape), full((1, D_MODEL))],
        out_specs=[row(D_MODEL), row(D_MODEL)],
        out_shape=[jax.ShapeDtypeStruct((t, D_MODEL), F32), jax.ShapeDtypeStruct((t, D_MODEL), BF16)],
        compiler_params=_cparams("parallel", "parallel"),
        name="merge",
    )(x, o_a, y_b_tm, o_m, gates, w_oa, w_ob, w_om, w_out, g2.reshape(1, D_MODEL).astype(F32))


def _ffn_kernel(x_ref, h_ref, wg_ref, wu_ref, wd_ref, o_ref, acc_sc):
    f = pl.program_id(1)

    @pl.when(f == 0)
    def _():
        acc_sc[...] = x_ref[...]

    h = h_ref[...]
    hg = jnp.dot(h, wg_ref[...], preferred_element_type=F32)
    hu = jnp.dot(h, wu_ref[...], preferred_element_type=F32)
    mid = (jax.nn.silu(hg) * hu).astype(BF16)
    acc_sc[...] += jnp.dot(mid, wd_ref[...], preferred_element_type=F32)

    @pl.when(f == pl.num_programs(1) - 1)
    def _():
        o_ref[...] = acc_sc[...]


def _ffn(x, hn, wg, wu, wd):
    t = x.shape[0]
    dff = wg.shape[1]
    tm = min(512, t)
    tf = dff // 2 if (dff // 2) % LANES == 0 else dff
    return pl.pallas_call(
        _ffn_kernel,
        grid=(t // tm, dff // tf),
        in_specs=[pl.BlockSpec((tm, D_MODEL), lambda i, f: (i, 0)), pl.BlockSpec((tm, D_MODEL), lambda i, f: (i, 0)),
                  pl.BlockSpec((D_MODEL, tf), lambda i, f: (0, f)), pl.BlockSpec((D_MODEL, tf), lambda i, f: (0, f)),
                  pl.BlockSpec((tf, D_MODEL), lambda i, f: (f, 0))],
        out_specs=pl.BlockSpec((tm, D_MODEL), lambda i, f: (i, 0)),
        out_shape=jax.ShapeDtypeStruct((t, D_MODEL), F32),
        scratch_shapes=[pltpu.VMEM((tm, D_MODEL), F32)],
        compiler_params=_cparams("parallel", "arbitrary"),
        name="ffn_dense",
    )(x, hn, wg, wu, wd)


def _router_kernel(h_ref, w_ref, tri_ref, meta_ref, cum_ref, tot_ref, carry_sc):
    i = pl.program_id(0)

    @pl.when(i == 0)
    def _():
        carry_sc[...] = jnp.zeros(carry_sc.shape, F32)

    lane = lax.broadcasted_iota(I32, (1, LANES), 1)
    logits = jnp.dot(h_ref[...], w_ref[...], preferred_element_type=F32)
    lg = jnp.where(lane < N_EXPERTS, logits, -jnp.inf)
    m1 = jnp.max(lg, axis=-1, keepdims=True)
    i1 = jnp.min(jnp.where(lg == m1, lane, LANES), axis=-1, keepdims=True)
    lg2 = jnp.where(lane == i1, -jnp.inf, lg)
    m2 = jnp.max(lg2, axis=-1, keepdims=True)
    i2 = jnp.min(jnp.where(lg2 == m2, lane, LANES), axis=-1, keepdims=True)
    e = jnp.exp(m2 - m1)
    g1 = 1.0 / (1.0 + e)
    g2 = e / (1.0 + e)
    hit1 = lane == i1
    hit2 = lane == i2
    cnt = jnp.where(hit1 | hit2, 1.0, 0.0)
    carry = carry_sc[...]
    before = jnp.dot(tri_ref[...], cnt.astype(BF16), preferred_element_type=F32) + carry
    r1 = jnp.sum(jnp.where(hit1, before, 0.0), axis=-1, keepdims=True)
    r2 = jnp.sum(jnp.where(hit2, before, 0.0), axis=-1, keepdims=True)
    cum_ref[0] = carry
    carry = carry + jnp.sum(cnt, axis=0, keepdims=True)
    carry_sc[...] = carry
    tot_ref[...] = carry
    meta = jnp.zeros(meta_ref.shape, F32)
    for c, val in enumerate((i1.astype(F32), i2.astype(F32), g1, g2, r1, r2)):
        meta = jnp.where(lane == c, val, meta)
    meta_ref[...] = meta


def _gather_kernel(ij_ref, is_ref, ifl_ref, h_ref, p0_ref, p1_ref, g0_ref, g1_ref, xs_ref, gs_ref, *, tg, ts):
    i = pl.program_id(0)
    fl = ifl_ref[i]

    @pl.when((fl & 2) != 0)
    def _():
        xs_ref[...] = jnp.zeros(xs_ref.shape, xs_ref.dtype)
        gs_ref[...] = jnp.zeros(gs_ref.shape, gs_ref.dtype)

    @pl.when((fl & 1) != 0)
    def _():
        rio = lax.broadcasted_iota(I32, (tg, ts), 0) + ij_ref[i] * tg
        m0 = p0_ref[0] == rio
        m1 = p1_ref[0] == rio
        sel = jnp.where(m0 | m1, 1.0, 0.0).astype(BF16)
        xs_ref[...] += jnp.dot(sel, h_ref[...], preferred_element_type=F32).astype(xs_ref.dtype)
        gs_ref[...] += jnp.sum(jnp.where(m0, g0_ref[0], 0.0) + jnp.where(m1, g1_ref[0], 0.0),
                               axis=-1, keepdims=True)


def _expert_kernel(be_ref, nv_ref, xs_ref, gs_ref, wg_ref, wu_ref, wd_ref, y_ref, acc_sc):
    j = pl.program_id(0)
    f = pl.program_id(1)
    valid = j < nv_ref[0]

    @pl.when(f == 0)
    def _():
        acc_sc[...] = jnp.zeros(acc_sc.shape, F32)

    @pl.when(valid)
    def _():
        x = xs_ref[...]
        hg = jnp.dot(x, wg_ref[...], preferred_element_type=F32)
        hu = jnp.dot(x, wu_ref[...], preferred_element_type=F32)
        mid = (jax.nn.silu(hg) * hu).astype(BF16)
        acc_sc[...] += jnp.dot(mid, wd_ref[...], preferred_element_type=F32)

    @pl.when(f == pl.num_programs(1) - 1)
    def _():
        y_ref[...] = jnp.where(valid, acc_sc[...] * gs_ref[...], 0.0).astype(y_ref.dtype)


def _combine_kernel(cj_ref, cs_ref, cfl_ref, x_ref, y_ref, p0_ref, p1_ref, o_ref, *, tg, ts):
    i = pl.program_id(0)
    fl = cfl_ref[i]

    @pl.when((fl & 2) != 0)
    def _():
        o_ref[...] = x_ref[...]

    @pl.when((fl & 1) != 0)
    def _():
        cio = lax.broadcasted_iota(I32, (ts, tg), 1) + cj_ref[i] * tg
        sel = jnp.where((p0_ref[...] == cio) | (p1_ref[...] == cio), 1.0, 0.0).astype(BF16)
        o_ref[...] += jnp.dot(sel, y_ref[...], preferred_element_type=F32)


def _moe(x, hn, router_w, wg, wu, wd):
    t = x.shape[0]
    dffe = wg.shape[2]
    ts = min(1024, t)
    tg = min(256, t)
    te = min(1024, t)
    tf = 512
    assert t % ts == 0 and te % tg == 0 and dffe % tf == 0
    nb = t // ts
    n_eblk = (2 * t) // te + N_EXPERTS
    ns = n_eblk * te
    n_gblk = ns // tg
    n_items = n_gblk + N_EXPERTS * nb

    w_pad = jnp.zeros((D_MODEL, LANES), BF16).at[:, :N_EXPERTS].set(router_w.astype(BF16))
    tri = (jnp.arange(ts)[:, None] > jnp.arange(ts)[None, :]).astype(BF16)
    meta, cum, tot = pl.pallas_call(
        _router_kernel,
        grid=(nb,),
        in_specs=[pl.BlockSpec((ts, D_MODEL), lambda i: (i, 0)), pl.BlockSpec((D_MODEL, LANES), lambda i: (0, 0)),
                  pl.BlockSpec((ts, ts), lambda i: (0, 0))],
        out_specs=[pl.BlockSpec((ts, LANES), lambda i: (i, 0)), pl.BlockSpec((1, 1, LANES), lambda i: (i, 0, 0)),
                   pl.BlockSpec((1, LANES), lambda i: (0, 0))],
        out_shape=[jax.ShapeDtypeStruct((t, LANES), F32), jax.ShapeDtypeStruct((nb, 1, LANES), F32),
                   jax.ShapeDtypeStruct((1, LANES), F32)],
        scratch_shapes=[pltpu.VMEM((1, LANES), F32)],
        compiler_params=_cparams("arbitrary"),
        name="moe_router",
    )(hn, w_pad, tri)

    e1 = meta[:, 0].astype(I32)
    e2 = meta[:, 1].astype(I32)
    g1 = meta[:, 2]
    g2 = meta[:, 3]
    r1 = meta[:, 4].astype(I32)
    r2 = meta[:, 5].astype(I32)
    counts = tot[0, :N_EXPERTS].astype(I32)
    cum_e = cum[:, 0, :N_EXPERTS].astype(I32)
    gpad = ((counts + te - 1) // te) * te
    gend = jnp.cumsum(gpad)
    gstart = gend - gpad
    pos0 = gstart[e1] + r1
    pos1 = gstart[e2] + r2
    nvalid_e = (gend[-1] // te).astype(I32)
    blk_e = jnp.arange(n_eblk, dtype=I32) * te
    be = jnp.minimum(jnp.sum(gend[None, :] <= blk_e[:, None], axis=1), N_EXPERTS - 1).astype(I32)
    be = jnp.where(jnp.arange(n_eblk) < nvalid_e, be, be[jnp.maximum(nvalid_e - 1, 0)])
    gb0 = jnp.arange(n_gblk, dtype=I32) * tg
    gb_valid = gb0 < gend[-1]
    gb_e = be[gb0 // te]
    rank0 = gb0 - gstart[gb_e]
    cnt_e = counts[gb_e]
    rank_last = jnp.maximum(jnp.minimum(rank0 + tg, cnt_e) - 1, 0)
    cum_cols = cum_e[:, gb_e]
    lo = jnp.sum(cum_cols <= jnp.minimum(rank0, rank_last)[None, :], axis=0) - 1
    hi = jnp.sum(cum_cols <= rank_last[None, :], axis=0) - 1
    n_it = jnp.where(gb_valid, hi - lo + 1, 0)
    it_end = jnp.cumsum(n_it)
    it_start = it_end - n_it
    total = it_end[-1]
    ii = jnp.arange(n_items, dtype=I32)
    live = ii < total
    iic = jnp.minimum(ii, total - 1)
    it_j = jnp.sum(it_end[None, :] <= iic[:, None], axis=1).astype(I32)
    it_s = (lo[it_j] + iic - it_start[it_j]).astype(I32)
    it_first = (iic == it_start[it_j]) & live
    it_fl = (live.astype(I32) + 2 * it_first.astype(I32)).astype(I32)
    order = jnp.argsort(jnp.where(live, it_s, nb), stable=True)
    cj = it_j[order]
    cs = it_s[order]
    cl = live[order]
    cs = jnp.where(cl, cs, cs[jnp.maximum(total - 1, 0)])
    cj = jnp.where(cl, cj, cj[jnp.maximum(total - 1, 0)])
    c_first = cl & jnp.concatenate([jnp.ones((1,), bool), cs[1:] != cs[:-1]])
    c_fl = (cl.astype(I32) + 2 * c_first.astype(I32)).astype(I32)

    row3 = lambda v: v.reshape(nb, 1, ts)
    col2 = lambda v: v.reshape(t, 1)

    gspec = pltpu.PrefetchScalarGridSpec(
        num_scalar_prefetch=3,
        grid=(n_items,),
        in_specs=[pl.BlockSpec((ts, D_MODEL), lambda i, ij, is_, fl: (is_[i], 0))]
        + [pl.BlockSpec((1, 1, ts), lambda i, ij, is_, fl: (is_[i], 0, 0)) for _ in range(4)],
        out_specs=[pl.BlockSpec((tg, D_MODEL), lambda i, ij, is_, fl: (ij[i], 0)),
                   pl.BlockSpec((tg, 1), lambda i, ij, is_, fl: (ij[i], 0))],
    )
    xs, gs = pl.pallas_call(
        functools.partial(_gather_kernel, tg=tg, ts=ts),
        grid_spec=gspec,
        out_shape=[jax.ShapeDtypeStruct((ns, D_MODEL), BF16), jax.ShapeDtypeStruct((ns, 1), F32)],
        compiler_params=_cparams("arbitrary"),
        name="moe_gather",
    )(it_j, it_s, it_fl, hn, row3(pos0), row3(pos1), row3(g1), row3(g2))

    espec = pltpu.PrefetchScalarGridSpec(
        num_scalar_prefetch=2,
        grid=(n_eblk, dffe // tf),
        in_specs=[pl.BlockSpec((te, D_MODEL), lambda j, f, be_, nv: (j, 0)),
                  pl.BlockSpec((te, 1), lambda j, f, be_, nv: (j, 0)),
                  pl.BlockSpec((None, D_MODEL, tf), lambda j, f, be_, nv: (be_[j], 0, f)),
                  pl.BlockSpec((None, D_MODEL, tf), lambda j, f, be_, nv: (be_[j], 0, f)),
                  pl.BlockSpec((None, tf, D_MODEL), lambda j, f, be_, nv: (be_[j], f, 0))],
        out_specs=pl.BlockSpec((te, D_MODEL), lambda j, f, be_, nv: (j, 0)),
        scratch_shapes=[pltpu.VMEM((te, D_MODEL), F32)],
    )
    y = pl.pallas_call(
        _expert_kernel,
        grid_spec=espec,
        out_shape=jax.ShapeDtypeStruct((ns, D_MODEL), BF16),
        compiler_params=_cparams("parallel", "arbitrary"),
        name="moe_experts",
    )(be, nvalid_e.reshape(1), xs, gs, wg, wu, wd)

    cspec = pltpu.PrefetchScalarGridSpec(
        num_scalar_prefetch=3,
        grid=(n_items,),
        in_specs=[pl.BlockSpec((ts, D_MODEL), lambda i, cj_, cs_, fl: (cs_[i], 0)),
                  pl.BlockSpec((tg, D_MODEL), lambda i, cj_, cs_, fl: (cj_[i], 0)),
                  pl.BlockSpec((ts, 1), lambda i, cj_, cs_, fl: (cs_[i], 0)),
                  pl.BlockSpec((ts, 1), lambda i, cj_, cs_, fl: (cs_[i], 0))],
        out_specs=pl.BlockSpec((ts, D_MODEL), lambda i, cj_, cs_, fl: (cs_[i], 0)),
    )
    return pl.pallas_call(
        functools.partial(_combine_kernel, tg=tg, ts=ts),
        grid_spec=cspec,
        out_shape=jax.ShapeDtypeStruct((t, D_MODEL), F32),
        compiler_params=_cparams("arbitrary"),
        name="moe_combine",
    )(cj, cs, c_fl, x, y, col2(pos0), col2(pos1))


def _layer(x, b, l, li, p, mem_k, mem_v, past):
    h = _rms(x, p["norm1"])
    w_in = p["w_in"]
    seg = lambda a: w_in[:, SPLITS[a]:(SPLITS[a + 1] if a + 1 < len(SPLITS) else None)]
    tile = lambda g, n: jnp.tile(g.astype(F32), n)
    (q,) = _mm(h, seg(0), epi="seg64", gain=tile(p["q_norm_a"], 2 * H_A) * (DH_A ** -0.5 * LOG2E),
               layouts=("head",), tn=512, bl=(b, l))
    k32, kbf = _mm(h, seg(1), epi="seg64", gain=tile(p["k_norm_a"], 2 * H_A), out_dtypes=(F32, BF16),
                   layouts=("tok", "head"), tn=512, bl=(b, l))
    v32, vbf = _mm(h, seg(2), out_dtypes=(F32, BF16), layouts=("tok", "head"), tn=512, bl=(b, l))
    (u_tm,) = _mm(h, seg(3), out_dtypes=(F32,), layouts=("time",), bl=(b, l))
    (qm,) = _mm(h, seg(4), epi="seg192", gain=tile(p["q_norm_m"], H_M) * (DH_M ** -0.5))
    (gates,) = _mm(h, seg(5), epi="sigmoid", tn=512)

    lam_init = 0.8 - 0.6 * math.exp(-0.3 * li)
    lam = (jnp.exp(jnp.sum(p["lam_q1"].astype(F32) * p["lam_k1"].astype(F32)))
           - jnp.exp(jnp.sum(p["lam_q2"].astype(F32) * p["lam_k2"].astype(F32))) + lam_init)
    if past is None:
        o_a = _attn_prompt(q, kbf, vbf, p["rel_bias"], p["subln_a"], lam, lam_init, b, l)
        zero = jnp.zeros((b, N_GROUPS, STATE_P), F32)
        y_b, s_re, s_im = _s5(u_tm, p, zero, zero, b, l)
    else:
        k_past, v_past, s0_re, s0_im = past
        o_a = _attn_sample(q, kbf, vbf, k_past, v_past, p["rel_bias"], p["subln_a"], lam, lam_init, b, l)
        y_b, s_re, s_im = _s5(u_tm, p, s0_re, s0_im, b, l)
    o_m = _memattn(qm, mem_k, mem_v, b, l)
    x, hn = _merge(x, o_a, y_b, o_m, gates, p["w_oa"], p["w_ob"], p["w_om"], p["w_out"], p["norm2"], b, l)
    if li % 2 == 0:
        x = _ffn(x, hn, p["ffn_w_gate"], p["ffn_w_up"], p["ffn_w_down"])
    else:
        x = _moe(x, hn, p["router"], p["moe_w_gate"], p["moe_w_up"], p["moe_w_down"])
    return x, k32, v32, s_re, s_im


def kernel(x_prompt, x_sample, mem_prompt, cache_attn_k, cache_attn_v, state_ssm_re, state_ssm_im, cache_mem_k, cache_mem_v, norm1, norm2, w_in, q_norm_a, k_norm_a, lam_q1, lam_k1, lam_q2, lam_k2, subln_a, w_oa, lambda_re, lambda_im, log_dt, b_re, b_im, c_re, c_im, d_skip, w_glu, b_glu, w_ob, w_mk, w_mv, q_norm_m, k_norm_m, w_om, w_out, rel_bias, ffn_w_gate, ffn_w_up, ffn_w_down, router, moe_w_gate, moe_w_up, moe_w_down):
    depth = w_in.shape[0]
    bp, lp, _ = x_prompt.shape
    bs, ls, _ = x_sample.shape
    n_mem = mem_prompt.shape[1]
    xp = x_prompt.reshape(bp * lp, D_MODEL)
    xs = x_sample.reshape(bs * ls, D_MODEL)
    mem_bf = mem_prompt.reshape(bp * n_mem, D_MODEL).astype(BF16)
    outs = {name: [] for name in ("kp", "vp", "srp", "sip", "mkp", "mvp", "ks", "vs", "srs", "sis")}
    for li in range(depth):
        j = li // 2
        p = {
            "norm1": norm1[li], "norm2": norm2[li], "w_in": w_in[li].astype(BF16),
            "q_norm_a": q_norm_a[li], "k_norm_a": k_norm_a[li],
            "lam_q1": lam_q1[li], "lam_k1": lam_k1[li], "lam_q2": lam_q2[li], "lam_k2": lam_k2[li],
            "subln_a": subln_a[li], "w_oa": w_oa[li].astype(BF16),
            "lambda_re": lambda_re[li], "lambda_im": lambda_im[li], "log_dt": log_dt[li],
            "b_re": b_re[li], "b_im": b_im[li], "c_re": c_re[li], "c_im": c_im[li],
            "d_skip": d_skip[li], "w_glu": w_glu[li], "b_glu": b_glu[li], "w_ob": w_ob[li].astype(BF16),
            "q_norm_m": q_norm_m[li], "w_om": w_om[li].astype(BF16), "w_out": w_out[li].astype(BF16),
            "rel_bias": rel_bias,
        }
        if li % 2 == 0:
            p.update(ffn_w_gate=ffn_w_gate[j].astype(BF16), ffn_w_up=ffn_w_up[j].astype(BF16),
                     ffn_w_down=ffn_w_down[j].astype(BF16))
        else:
            p.update(router=router[j], moe_w_gate=moe_w_gate[j].astype(BF16), moe_w_up=moe_w_up[j].astype(BF16),
                     moe_w_down=moe_w_down[j].astype(BF16))
        mk32, mkbf = _mm(mem_bf, w_mk[li].astype(BF16), epi="seg192", gain=jnp.tile(k_norm_m[li].astype(F32), H_M),
                         out_dtypes=(F32, BF16))
        mv32, mvbf = _mm(mem_bf, w_mv[li].astype(BF16), out_dtypes=(F32, BF16))
        xp, k_p, v_p, sr_p, si_p = _layer(xp, bp, lp, li, p, mkbf.reshape(bp, n_mem, W_M),
                                          mvbf.reshape(bp, n_mem, W_M), None)
        past = (cache_attn_k[li], cache_attn_v[li], state_ssm_re[li], state_ssm_im[li])
        xs, k_s, v_s, sr_s, si_s = _layer(xs, bs, ls, li, p,
                                          cache_mem_k[li].reshape(bs, n_mem, W_M).astype(BF16),
                                          cache_mem_v[li].reshape(bs, n_mem, W_M).astype(BF16), past)
        outs["kp"].append(k_p.reshape(bp, lp, H_A, DV_A))
        outs["vp"].append(v_p.reshape(bp, lp, H_A, DV_A))
        outs["srp"].append(sr_p)
        outs["sip"].append(si_p)
        outs["mkp"].append(mk32.reshape(bp, n_mem, H_M, DH_M))
        outs["mvp"].append(mv32.reshape(bp, n_mem, H_M, DH_M))
        outs["ks"].append(k_s.reshape(bs, ls, H_A, DV_A))
        outs["vs"].append(v_s.reshape(bs, ls, H_A, DV_A))
        outs["srs"].append(sr_s)
        outs["sis"].append(si_s)
    st = lambda name: jnp.stack(outs[name])
    return (xp.reshape(bp, lp, D_MODEL), xs.reshape(bs, ls, D_MODEL), st("kp"), st("vp"), st("srp"), st("sip"),
            st("mkp"), st("mvp"), st("ks"), st("vs"), st("srs"), st("sis"))
```

```python
import functools
import math

import jax
import jax.numpy as jnp
from jax import lax
from jax.experimental import pallas as pl
from jax.experimental.pallas import tpu as pltpu

F32 = jnp.float32
BF16 = jnp.bfloat16
I32 = jnp.int32

D_MODEL = 1024
CHUNK = 64
H_A = 8
DH_A = 64
DV_A = 2 * DH_A
W_A = H_A * DV_A
GROUP_CH = 16
N_GROUPS = 48
STATE_P = 64
W_B = N_GROUPS * GROUP_CH
H_M = 4
DH_M = 192
W_M = H_M * DH_M
DH_M_PAD = 256
W_M_PAD = H_M * DH_M_PAD
REL_BUCKETS = 32
REL_MAX_DIST = 128
N_EXPERTS = 8
ROUTER_FIELDS = 8
EPS = 1e-6
NEG_INF = -1e30
LOG2E = math.log2(math.e)
Q_A_W = H_A * 2 * DH_A
SPLITS = (0, Q_A_W, 2 * Q_A_W, 2 * Q_A_W + W_A, 2 * Q_A_W + W_A + W_B, 2 * Q_A_W + W_A + W_B + W_M)
GATE_W = 3 * D_MODEL

LANES = 128
VMEM_LIMIT = 56 * 1024 * 1024
ATTN_GROUP = 4
S5_SLAB_GROUPS = LANES // GROUP_CH
S5_SLABS = N_GROUPS // S5_SLAB_GROUPS
S5_SLAB_STATE = S5_SLAB_GROUPS * STATE_P


def _cparams(*sem):
    return pltpu.CompilerParams(dimension_semantics=sem, vmem_limit_bytes=VMEM_LIMIT)


def _rms_kernel(x_ref, g_ref, o_ref):
    x = x_ref[...]
    ms = jnp.mean(x * x, axis=-1, keepdims=True)
    o_ref[...] = ((x * lax.rsqrt(ms + EPS)) * g_ref[...]).astype(o_ref.dtype)


def _rms(x, g):
    t, d = x.shape
    tm = min(1024, t)
    return pl.pallas_call(
        _rms_kernel,
        grid=(t // tm,),
        in_specs=[pl.BlockSpec((tm, d), lambda i: (i, 0)), pl.BlockSpec((1, d), lambda i: (0, 0))],
        out_specs=pl.BlockSpec((tm, d), lambda i: (i, 0)),
        out_shape=jax.ShapeDtypeStruct((t, d), BF16),
        compiler_params=_cparams("parallel"),
        name="rmsnorm",
    )(x, g.reshape(1, d).astype(F32))


def _seg64_scale(ys):
    lane = lax.broadcasted_iota(I32, (1, LANES), 1)
    left = lane < DH_A
    y2 = ys * ys
    sl = jnp.sum(jnp.where(left, y2, 0.0), axis=-1, keepdims=True)
    sr = jnp.sum(jnp.where(left, 0.0, y2), axis=-1, keepdims=True)
    rl = lax.rsqrt(sl * (1.0 / DH_A) + EPS)
    rr = lax.rsqrt(sr * (1.0 / DH_A) + EPS)
    return jnp.where(left, rl, rr)


def _seg192_scale(y):
    col = lax.broadcasted_iota(I32, (1, W_M), 1)
    y2 = y * y
    rb = jnp.zeros_like(y)
    for h in range(H_M):
        m = (col >= DH_M * h) & (col < DH_M * (h + 1))
        s = jnp.sum(jnp.where(m, y2, 0.0), axis=-1, keepdims=True)
        rb = jnp.where(m, lax.rsqrt(s / DH_M + EPS), rb)
    return rb


def _mm_kernel(*refs, epi, layouts, has_gain):
    a_ref, w_ref = refs[0], refs[1]
    g_ref = refs[2] if has_gain else None
    outs = refs[len(refs) - len(layouts):]
    y = jnp.dot(a_ref[...], w_ref[...], preferred_element_type=F32)
    tn = y.shape[1]
    if epi == "seg192":
        y = (y * _seg192_scale(y)) * g_ref[...]
    elif epi == "pad192":
        for h in range(tn // DH_M_PAD):
            sl = slice(h * DH_M_PAD, (h + 1) * DH_M_PAD)
            ys = y[:, sl]
            r = lax.rsqrt(jnp.sum(ys * ys, axis=-1, keepdims=True) / DH_M + EPS)
            for o in outs:
                o[:, sl] = ((ys * r) * g_ref[:, sl]).astype(o.dtype)
        return
    elif epi == "sigmoid":
        y = jax.nn.sigmoid(y)
    if epi == "seg64" or "head" in layouts:
        for s in range(tn // LANES):
            sl = slice(s * LANES, (s + 1) * LANES)
            ys = y[:, sl]
            if epi == "seg64":
                ys = (ys * _seg64_scale(ys)) * g_ref[:, sl]
            for o, lay in zip(outs, layouts):
                if lay == "head":
                    o[s] = ys.astype(o.dtype)
                else:
                    o[:, sl] = ys.astype(o.dtype)
        return
    for o in outs:
        o[...] = y.astype(o.dtype)


def _mm(a, w, *, epi="plain", gain=None, out_dtypes=(BF16,), layouts=None, tn=None, bl=None, stack=None):
    t, k = a.shape
    n = w.shape[1]
    tn = n if tn is None else tn
    layouts = ("tok",) * len(out_dtypes) if layouts is None else layouts
    assert n % tn == 0 and tn % LANES == 0
    if bl is None:
        assert all(lay in ("tok", "stack") for lay in layouts)
        tm, nl = min(1024, t), 1
    else:
        b, l = bl
        tm = min(1024, l)
        assert l % tm == 0
        nl = l // tm
    assert t % tm == 0
    out_specs, out_shapes = [], []
    alias_args, aliases = [], {}
    n_in = 2 + int(gain is not None)
    for oi, (dt, lay) in enumerate(zip(out_dtypes, layouts)):
        if lay == "tok":
            out_specs.append(pl.BlockSpec((tm, tn), lambda i, j: (i, j)))
            out_shapes.append(jax.ShapeDtypeStruct((t, n), dt))
        elif lay == "stack":
            li, depth, buf = stack
            out_specs.append(pl.BlockSpec((None, tm, tn), lambda i, j: (li, i, j)))
            out_shapes.append(jax.ShapeDtypeStruct((depth, t, n), dt))
            if buf is not None:
                aliases[n_in + len(alias_args)] = oi
                alias_args.append(buf)
        elif lay == "time":
            assert tn == n
            out_specs.append(pl.BlockSpec((tm, tn), lambda i, j: (i % nl, i // nl)))
            out_shapes.append(jax.ShapeDtypeStruct((l, b * n), dt))
        else:
            out_specs.append(pl.BlockSpec((None, tn // LANES, tm, LANES), lambda i, j: (i // nl, j, i % nl, 0)))
            out_shapes.append(jax.ShapeDtypeStruct((b, n // LANES, l, LANES), dt))
    in_specs = [pl.BlockSpec((tm, k), lambda i, j: (i, 0)), pl.BlockSpec((k, tn), lambda i, j: (0, j))]
    args = [a, w]
    if gain is not None:
        in_specs.append(pl.BlockSpec((1, tn), lambda i, j: (0, j)))
        args.append(gain.reshape(1, n).astype(F32))
    in_specs += [pl.BlockSpec(memory_space=pl.ANY) for _ in alias_args]
    return pl.pallas_call(
        functools.partial(_mm_kernel, epi=epi, layouts=tuple(layouts), has_gain=gain is not None),
        grid=(t // tm, n // tn),
        in_specs=in_specs,
        out_specs=out_specs,
        out_shape=out_shapes,
        input_output_aliases=aliases,
        compiler_params=_cparams("parallel", "parallel"),
        name="mm_" + epi,
    )(*args, *alias_args)


def _rel_bias(q_pos, k_pos, table):
    rel = k_pos[None, :] - q_pos[:, None]
    half = REL_BUCKETS // 2
    max_exact = half // 2
    n = jnp.abs(rel)
    nf = jnp.maximum(n, 1).astype(F32)
    large = max_exact + (jnp.log(nf / max_exact) / math.log(REL_MAX_DIST / max_exact)
                         * (half - max_exact)).astype(I32)
    large = jnp.minimum(large, half - 1)
    bucket = jnp.where(rel > 0, half, 0) + jnp.where(n < max_exact, n, large)
    bias = jnp.transpose(table[bucket].astype(F32), (2, 0, 1))
    visible = (k_pos[None, :] // CHUNK) <= (q_pos[:, None] // CHUNK)
    return jnp.where(visible[None], bias, NEG_INF)


def _split_halves(q):
    lane = lax.broadcasted_iota(I32, (1, LANES), 1)
    zero = jnp.zeros_like(q)
    return jnp.concatenate([jnp.where(lane < DH_A, q, zero), jnp.where(lane >= DH_A, q, zero)], axis=0)


def _subln(o, g, lam_init):
    ms = jnp.mean(o * o, axis=-1, keepdims=True)
    return ((o * lax.rsqrt(ms + EPS)) * g) * (1.0 - lam_init)


def _attn_prompt_kernel(lam_ref, q_ref, k_ref, v_ref, bias_ref, g_ref, o_ref,
                        s_sc, m_sc, l_sc, acc_sc, *, tq, lam_init):
    qi = pl.program_id(2)
    n_blk = qi + 1
    qs = _split_halves(q_ref[...])
    nt = (((1,), (1,)), ((), ()))
    nc = tq // LANES

    def fold(op, s):
        r = s[:, :LANES]
        for c in range(1, nc):
            r = op(r, s[:, c * LANES:(c + 1) * LANES])
        return r

    def score(j):
        r0 = pl.multiple_of(j * tq, tq)
        bias = bias_ref[jnp.clip(j - (qi - 2), 0, 2)]
        s = lax.dot_general(qs, k_ref[pl.ds(r0, tq), :], nt, preferred_element_type=F32)
        s = s + jnp.concatenate([bias, bias], axis=0)
        s_sc[j] = s
        return fold(jnp.maximum, s)

    def sweep(group_fn):
        n_full = n_blk // ATTN_GROUP

        def trip(g, c):
            group_fn(ATTN_GROUP * g, ATTN_GROUP)
            return c

        lax.fori_loop(0, n_full, trip, 0)
        rem = n_blk - ATTN_GROUP * n_full

        @pl.when(rem >= 2)
        def _():
            group_fn(ATTN_GROUP * n_full, 2)

        @pl.when(rem % 2 == 1)
        def _():
            group_fn(qi, 1)

    m_sc[...] = jnp.full(m_sc.shape, -jnp.inf, F32)

    def score_group(j0, g):
        m_sc[...] = jnp.maximum(m_sc[...], functools.reduce(jnp.maximum, [score(j0 + t) for t in range(g)]))

    sweep(score_group)
    m = jnp.max(m_sc[...], axis=-1, keepdims=True)
    m_sc[...] = jnp.broadcast_to(m, m_sc.shape)
    l_sc[...] = jnp.zeros(l_sc.shape, F32)
    acc_sc[...] = jnp.zeros(acc_sc.shape, F32)

    def probs(j):
        mrow = m_sc[...]
        s = s_sc[j]
        p = jnp.concatenate([jnp.exp2(s[:, cc * LANES:(cc + 1) * LANES] - mrow) for cc in range(nc)], axis=1)
        r0 = pl.multiple_of(j * tq, tq)
        return fold(jnp.add, p), jnp.dot(p.astype(BF16), v_ref[pl.ds(r0, tq), :], preferred_element_type=F32)

    def pv_group(j0, g):
        parts = [probs(j0 + t) for t in range(g)]
        l_sc[...] += functools.reduce(jnp.add, [pt[0] for pt in parts])
        acc_sc[...] += functools.reduce(jnp.add, [pt[1] for pt in parts])

    sweep(pv_group)
    o = acc_sc[...] / jnp.sum(l_sc[...], axis=-1, keepdims=True)
    o = o[:tq] - lam_ref[0] * o[tq:]
    o_ref[...] = _subln(o, g_ref[...], lam_init).astype(o_ref.dtype)


def _attn_prompt(q, k, v, rel_table, subln_g, lam, lam_init, b, l):
    tq = min(512, l)
    assert l % tq == 0 and tq % CHUNK == 0 and tq >= 128
    cfar = _rel_bias(jnp.array([2 * tq + 1]), jnp.array([0]), rel_table)[:, 0, 0]
    bias = _rel_bias(jnp.arange(tq, 2 * tq), jnp.arange(2 * tq), rel_table)
    bias = (bias - cfar[:, None, None]) * LOG2E
    bias = jnp.stack([jnp.zeros_like(bias[:, :, :tq]), bias[:, :, :tq], bias[:, :, tq:]], axis=1)
    smem = pl.BlockSpec(memory_space=pltpu.SMEM)
    out = pl.pallas_call(
        functools.partial(_attn_prompt_kernel, tq=tq, lam_init=lam_init),
        grid=(b, H_A, l // tq),
        in_specs=[smem,
                  pl.BlockSpec((None, None, tq, DV_A), lambda bi, h, i: (bi, h, i, 0)),
                  pl.BlockSpec((None, None, l, DV_A), lambda bi, h, i: (bi, h, 0, 0)),
                  pl.BlockSpec((None, None, l, DV_A), lambda bi, h, i: (bi, h, 0, 0)),
                  pl.BlockSpec((None, 3, tq, tq), lambda bi, h, i: (h, 0, 0, 0)),
                  pl.BlockSpec((1, DV_A), lambda bi, h, i: (0, 0))],
        out_specs=pl.BlockSpec((None, tq, DV_A), lambda bi, h, i: (bi, i, h)),
        out_shape=jax.ShapeDtypeStruct((b, l, W_A), BF16),
        scratch_shapes=[pltpu.VMEM((l // tq, 2 * tq, tq), F32), pltpu.VMEM((2 * tq, LANES), F32),
                        pltpu.VMEM((2 * tq, LANES), F32), pltpu.VMEM((2 * tq, DV_A), F32)],
        compiler_params=_cparams("parallel", "parallel", "arbitrary"),
        name="attn_prompt",
    )(lam.reshape(1), q, k, v, bias, subln_g.reshape(1, DV_A).astype(F32))
    return out.reshape(b * l, W_A)


def _attn_sample_kernel(lam_ref, q_ref, kp_ref, vp_ref, kn_ref, vn_ref, bp_ref, bn_ref, g_ref, o_ref,
                        *, lq, lam_init):
    lam = lam_ref[0]
    nt = (((1,), (1,)), ((), ()))
    for h in range(H_A):
        sl = slice(h * DV_A, (h + 1) * DV_A)
        qs = _split_halves(q_ref[h])
        kp = kp_ref[:, sl].astype(BF16)
        vp = vp_ref[:, sl].astype(BF16)
        bp = bp_ref[h]
        bn = bn_ref[h]
        sp = lax.dot_general(qs, kp, nt, preferred_element_type=F32) + jnp.concatenate([bp, bp], axis=0)
        sn = lax.dot_general(qs, kn_ref[h], nt, preferred_element_type=F32) + jnp.concatenate([bn, bn], axis=0)
        m = jnp.maximum(jnp.max(sp, axis=-1, keepdims=True), jnp.max(sn, axis=-1, keepdims=True))
        pp = jnp.exp2(sp - m)
        pn = jnp.exp2(sn - m)
        lsum = jnp.sum(pp, axis=-1, keepdims=True) + jnp.sum(pn, axis=-1, keepdims=True)
        o = (jnp.dot(pp.astype(BF16), vp, preferred_element_type=F32)
             + jnp.dot(pn.astype(BF16), vn_ref[h], preferred_element_type=F32)) / lsum
        o = o[:lq] - lam * o[lq:]
        o_ref[:, sl] = _subln(o, g_ref[...], lam_init).astype(o_ref.dtype)


def _attn_sample(q, k_new, v_new, k_past, v_past, rel_table, subln_g, lam, lam_init, b, l):
    past = k_past.shape[1]
    k_pos = jnp.arange(past + l)
    bias = _rel_bias(k_pos[past:], k_pos, rel_table) * LOG2E
    bp, bn = bias[:, :, :past], bias[:, :, past:]
    smem = pl.BlockSpec(memory_space=pltpu.SMEM)
    tok = lambda rows: pl.BlockSpec((None, rows, W_A), lambda bi: (bi, 0, 0))
    head = pl.BlockSpec((None, H_A, l, DV_A), lambda bi: (bi, 0, 0, 0))
    full = lambda shape: pl.BlockSpec(shape, lambda bi: tuple(0 for _ in shape))
    out = pl.pallas_call(
        functools.partial(_attn_sample_kernel, lq=l, lam_init=lam_init),
        grid=(b,),
        in_specs=[smem, head, tok(past), tok(past), head, head,
                  full((H_A, l, past)), full((H_A, l, l)), full((1, DV_A))],
        out_specs=tok(l),
        out_shape=jax.ShapeDtypeStruct((b, l, W_A), BF16),
        compiler_params=_cparams("parallel"),
        name="attn_sample",
    )(lam.reshape(1), q, k_past.reshape(b, past, W_A), v_past.reshape(b, past, W_A),
      k_new, v_new, bp, bn, subln_g.reshape(1, DV_A).astype(F32))
    return out.reshape(b * l, W_A)


def _pad_heads(z):
    lead = z.shape[:-1]
    z = z.reshape(*lead, H_M, DH_M)
    z = jnp.pad(z, [(0, 0)] * len(lead) + [(0, 0), (0, DH_M_PAD - DH_M)])
    return z.reshape(*lead, W_M_PAD)


def _memattn_kernel(q_ref, mk_ref, mv_ref, o_ref):
    nt = (((1,), (1,)), ((), ()))
    for h in range(H_M):
        sl = slice(h * DH_M_PAD, (h + 1) * DH_M_PAD)
        s = lax.dot_general(q_ref[:, sl], mk_ref[:, sl], nt, preferred_element_type=F32)
        m = jnp.max(s, axis=-1, keepdims=True)
        p = jnp.exp(s - m)
        p = p / jnp.sum(p, axis=-1, keepdims=True)
        o_ref[:, sl] = jnp.dot(p.astype(BF16), mv_ref[:, sl], preferred_element_type=F32).astype(o_ref.dtype)


def _memattn(qm, mk, mv, b, l):
    n_mem = mk.shape[1]
    tq = min(512, l)
    assert l % tq == 0
    out = pl.pallas_call(
        _memattn_kernel,
        grid=(b, l // tq),
        in_specs=[pl.BlockSpec((None, tq, W_M_PAD), lambda bi, i: (bi, i, 0)),
                  pl.BlockSpec((None, n_mem, W_M_PAD), lambda bi, i: (bi, 0, 0)),
                  pl.BlockSpec((None, n_mem, W_M_PAD), lambda bi, i: (bi, 0, 0))],
        out_specs=pl.BlockSpec((None, tq, W_M_PAD), lambda bi, i: (bi, i, 0)),
        out_shape=jax.ShapeDtypeStruct((b, l, W_M_PAD), BF16),
        compiler_params=_cparams("parallel", "parallel"),
        name="memattn",
    )(qm.reshape(b, l, W_M_PAD), mk, mv)
    return out.reshape(b * l, W_M_PAD)


def _s5_tables(p):
    dt = jnp.exp(p["log_dt"].astype(F32))[:, None]
    lr = jnp.minimum(p["lambda_re"].astype(F32), -1e-4)
    lim = p["lambda_im"].astype(F32)
    mag = jnp.exp(lr * dt)
    ar = mag * jnp.cos(lim * dt)
    ai = mag * jnp.sin(lim * dt)
    den = lr * lr + lim * lim
    fr = ((ar - 1.0) * lr + ai * lim) / den
    fi = (ai * lr - (ar - 1.0) * lim) / den
    br = p["b_re"].astype(F32)
    bi = p["b_im"].astype(F32)
    bbr = fr[..., None] * br - fi[..., None] * bi
    bbi = fr[..., None] * bi + fi[..., None] * br
    eye = jnp.eye(S5_SLAB_GROUPS, dtype=F32)
    sg = (S5_SLABS, S5_SLAB_GROUPS)

    def in_w(bb):
        w = jnp.einsum("sgpc,gh->sgchp", bb.reshape(*sg, STATE_P, GROUP_CH), eye)
        return w.reshape(S5_SLABS, LANES, S5_SLAB_STATE)

    def out_w(c):
        w = jnp.einsum("sgcp,gh->sgphc", c.reshape(*sg, GROUP_CH, STATE_P), eye)
        return w.reshape(S5_SLABS, S5_SLAB_STATE, LANES)

    wb = jnp.concatenate([in_w(bbr), in_w(bbi)], axis=2).astype(BF16)
    wc = jnp.concatenate([out_w(p["c_re"].astype(F32)), -out_w(p["c_im"].astype(F32))], axis=1).astype(BF16)
    a_re = ar.reshape(S5_SLABS, 1, S5_SLAB_STATE)
    a_im = ai.reshape(S5_SLABS, 1, S5_SLAB_STATE)
    return wb, wc, a_re, a_im


def _s5_kernel(u_ref, wb_ref, wc_ref, ar_ref, ai_ref, d_ref, wglu_ref, bglu_ref, s0r_ref, s0i_ref,
               y_ref, sr_ref, si_ref, bu_sc, x_sc, y_sc, *, tl, nb):
    i = pl.program_id(0)

    @pl.when(i == 0)
    def _():
        sr_ref[...] = s0r_ref[...]
        si_ref[...] = s0i_ref[...]

    rows = tl * nb
    u = u_ref[...].reshape(rows, W_B)
    y_sc[...] = d_ref[...] * u
    for s in range(S5_SLABS):
        sl = slice(s * LANES, (s + 1) * LANES)
        bu_sc[...] = jnp.dot(u[:, sl].astype(BF16), wb_ref[s], preferred_element_type=F32)
        ar = jnp.broadcast_to(ar_ref[s], (nb, S5_SLAB_STATE))
        ai = jnp.broadcast_to(ai_ref[s], (nb, S5_SLAB_STATE))

        def body(t, carry):
            sr, si = carry
            r0 = pl.multiple_of(t * nb, nb)
            b_r = bu_sc[pl.ds(r0, nb), :S5_SLAB_STATE]
            b_i = bu_sc[pl.ds(r0, nb), S5_SLAB_STATE:]
            nr = ar * sr - ai * si + b_r
            ni = ar * si + ai * sr + b_i
            x_sc[pl.ds(r0, nb), :S5_SLAB_STATE] = nr.astype(BF16)
            x_sc[pl.ds(r0, nb), S5_SLAB_STATE:] = ni.astype(BF16)
            return nr, ni

        sr, si = lax.fori_loop(0, tl, body, (sr_ref[s], si_ref[s]))
        sr_ref[s] = sr
        si_ref[s] = si
        y_sc[:, sl] = y_sc[:, sl] + jnp.dot(x_sc[...], wc_ref[s], preferred_element_type=F32)
    yg = jax.nn.gelu(y_sc[...])
    z = jnp.dot(yg.astype(BF16), wglu_ref[...], preferred_element_type=F32) + bglu_ref[...]
    out = yg * jax.nn.sigmoid(z)
    y_ref[...] = out.astype(y_ref.dtype).reshape(tl, nb, W_B)


def _s5(u_tm, p, s0_re, s0_im, b, l):
    assert b % 16 == 0
    tl = min(64, l)
    assert l % tl == 0
    wb, wc, a_re, a_im = _s5_tables(p)
    to_slab = lambda s0: jnp.transpose(s0.astype(F32).reshape(b, S5_SLABS, S5_SLAB_STATE), (1, 0, 2))
    from_slab = lambda st: jnp.transpose(st, (1, 0, 2)).reshape(b, N_GROUPS, STATE_P)
    full = lambda shape: pl.BlockSpec(shape, lambda i: tuple(0 for _ in shape))
    st_shape = (S5_SLABS, b, S5_SLAB_STATE)
    rows = tl * b
    y, sr, si = pl.pallas_call(
        functools.partial(_s5_kernel, tl=tl, nb=b),
        grid=(l // tl,),
        in_specs=[pl.BlockSpec((tl, b, W_B), lambda i: (i, 0, 0)),
                  full(wb.shape), full(wc.shape), full(a_re.shape), full(a_im.shape),
                  full((1, W_B)), full((W_B, W_B)), full((1, W_B)), full(st_shape), full(st_shape)],
        out_specs=[pl.BlockSpec((tl, b, W_B), lambda i: (i, 0, 0)), full(st_shape), full(st_shape)],
        out_shape=[jax.ShapeDtypeStruct((l, b, W_B), BF16),
                   jax.ShapeDtypeStruct(st_shape, F32), jax.ShapeDtypeStruct(st_shape, F32)],
        scratch_shapes=[pltpu.VMEM((rows, 2 * S5_SLAB_STATE), F32),
                        pltpu.VMEM((rows, 2 * S5_SLAB_STATE), BF16),
                        pltpu.VMEM((rows, W_B), F32)],
        compiler_params=_cparams("arbitrary"),
        name="s5_scan",
    )(u_tm.reshape(l, b, W_B), wb, wc, a_re, a_im,
      p["d_skip"].astype(F32).reshape(1, W_B), p["w_glu"].astype(BF16), p["b_glu"].astype(F32).reshape(1, W_B),
      to_slab(s0_re), to_slab(s0_im))
    return y.reshape(l, b * W_B), from_slab(sr), from_slab(si)


def _merge_kernel(x_ref, oa_ref, yb_ref, om_ref, gt_ref, woa_ref, wob_ref, wom_ref, wout_ref, g2_ref,
                  xo_ref, hn_ref):
    pa = jnp.dot(oa_ref[...], woa_ref[...], preferred_element_type=F32)
    pb = jnp.dot(yb_ref[...], wob_ref[...], preferred_element_type=F32)
    pm = jnp.dot(om_ref[...], wom_ref[...], preferred_element_type=F32)
    d = D_MODEL
    mix = (gt_ref[:, :d].astype(F32) * pa + gt_ref[:, d:2 * d].astype(F32) * pb
           + gt_ref[:, 2 * d:].astype(F32) * pm)
    x = x_ref[...] + jnp.dot(mix.astype(BF16), wout_ref[...], preferred_element_type=F32)
    xo_ref[...] = x
    ms = jnp.mean(x * x, axis=-1, keepdims=True)
    hn_ref[...] = ((x * lax.rsqrt(ms + EPS)) * g2_ref[...]).astype(hn_ref.dtype)


def _merge(x, o_a, y_b_tm, o_m, gates, w_oa, w_ob, w_om, w_out, g2, b, l):
    t = b * l
    tm = min(512, l)
    assert l % tm == 0
    nl = l // tm
    row = lambda w: pl.BlockSpec((tm, w), lambda bi, i: (bi * nl + i, 0))
    full = lambda shape: pl.BlockSpec(shape, lambda bi, i: (0, 0))
    return pl.pallas_call(
        _merge_kernel,
        grid=(b, nl),
        in_specs=[row(D_MODEL), row(W_A), pl.BlockSpec((tm, W_B), lambda bi, i: (i, bi)), row(W_M_PAD), row(GATE_W),
                  full(w_oa.shape), full(w_ob.shape), full(w_om.shape), full(w_out.shape), full((1, D_MODEL))],
        out_specs=[row(D_MODEL), row(D_MODEL)],
        out_shape=[jax.ShapeDtypeStruct((t, D_MODEL), F32), jax.ShapeDtypeStruct((t, D_MODEL), BF16)],
        compiler_params=_cparams("parallel", "parallel"),
        name="merge",
    )(x, o_a, y_b_tm, o_m, gates, w_oa, w_ob, w_om, w_out, g2.reshape(1, D_MODEL).astype(F32))


def _ffn_kernel(x_ref, h_ref, wg_ref, wu_ref, wd_ref, o_ref, acc_sc):
    f = pl.program_id(1)

    @pl.when(f == 0)
    def _():
        acc_sc[...] = x_ref[...]

    h = h_ref[...]
    hg = jnp.dot(h, wg_ref[...], preferred_element_type=F32)
    hu = jnp.dot(h, wu_ref[...], preferred_element_type=F32)
    mid = (jax.nn.silu(hg) * hu).astype(BF16)
    acc_sc[...] += jnp.dot(mid, wd_ref[...], preferred_element_type=F32)

    @pl.when(f == pl.num_programs(1) - 1)
    def _():
        o_ref[...] = acc_sc[...]


def _ffn(x, hn, wg, wu, wd):
    t = x.shape[0]
    dff = wg.shape[1]
    tm = min(512, t)
    tf = dff // 2 if (dff // 2) % LANES == 0 else dff
    return pl.pallas_call(
        _ffn_kernel,
        grid=(t // tm, dff // tf),
        in_specs=[pl.BlockSpec((tm, D_MODEL), lambda i, f: (i, 0)), pl.BlockSpec((tm, D_MODEL), lambda i, f: (i, 0)),
                  pl.BlockSpec((D_MODEL, tf), lambda i, f: (0, f)), pl.BlockSpec((D_MODEL, tf), lambda i, f: (0, f)),
                  pl.BlockSpec((tf, D_MODEL), lambda i, f: (f, 0))],
        out_specs=pl.BlockSpec((tm, D_MODEL), lambda i, f: (i, 0)),
        out_shape=jax.ShapeDtypeStruct((t, D_MODEL), F32),
        scratch_shapes=[pltpu.VMEM((tm, D_MODEL), F32)],
        compiler_params=_cparams("parallel", "arbitrary"),
        name="ffn_dense",
    )(x, hn, wg, wu, wd)


def _router_kernel(h_ref, w_ref, tri_ref, meta_ref, cum_ref, tot_ref, carry_sc):
    i = pl.program_id(0)

    @pl.when(i == 0)
    def _():
        carry_sc[...] = jnp.zeros(carry_sc.shape, F32)

    lane = lax.broadcasted_iota(I32, (1, LANES), 1)
    logits = jnp.dot(h_ref[...], w_ref[...], preferred_element_type=F32)
    lg = jnp.where(lane < N_EXPERTS, logits, -jnp.inf)
    m1 = jnp.max(lg, axis=-1, keepdims=True)
    i1 = jnp.min(jnp.where(lg == m1, lane, LANES), axis=-1, keepdims=True)
    lg2 = jnp.where(lane == i1, -jnp.inf, lg)
    m2 = jnp.max(lg2, axis=-1, keepdims=True)
    i2 = jnp.min(jnp.where(lg2 == m2, lane, LANES), axis=-1, keepdims=True)
    e = jnp.exp(m2 - m1)
    g1 = 1.0 / (1.0 + e)
    g2 = e / (1.0 + e)
    hit1 = lane == i1
    hit2 = lane == i2
    cnt = jnp.where(hit1 | hit2, 1.0, 0.0)
    carry = carry_sc[...]
    before = jnp.dot(tri_ref[...], cnt.astype(BF16), preferred_element_type=F32) + carry
    r1 = jnp.sum(jnp.where(hit1, before, 0.0), axis=-1, keepdims=True)
    r2 = jnp.sum(jnp.where(hit2, before, 0.0), axis=-1, keepdims=True)
    cum_ref[0] = carry
    carry = carry + jnp.sum(cnt, axis=0, keepdims=True)
    carry_sc[...] = carry
    tot_ref[...] = carry
    meta = jnp.zeros(logits.shape, F32)
    for c, val in enumerate((i1.astype(F32), i2.astype(F32), g1, g2, r1, r2)):
        meta = jnp.where(lane == c, val, meta)
    meta_ref[0] = meta.T[:ROUTER_FIELDS]


def _gather_kernel(ij_ref, is_ref, ifl_ref, h_ref, p0_ref, p1_ref, g0_ref, g1_ref, xs_ref, gs_ref, *, tg, ts):
    i = pl.program_id(0)
    fl = ifl_ref[i]

    @pl.when((fl & 2) != 0)
    def _():
        xs_ref[...] = jnp.zeros(xs_ref.shape, xs_ref.dtype)
        gs_ref[...] = jnp.zeros(gs_ref.shape, gs_ref.dtype)

    @pl.when((fl & 1) != 0)
    def _():
        rio = lax.broadcasted_iota(I32, (tg, ts), 0) + ij_ref[i] * tg
        m0 = p0_ref[0] == rio
        m1 = p1_ref[0] == rio
        sel = jnp.where(m0 | m1, 1.0, 0.0).astype(BF16)
        xs_ref[...] += jnp.dot(sel, h_ref[...], preferred_element_type=F32).astype(xs_ref.dtype)
        gs_ref[...] += jnp.sum(jnp.where(m0, g0_ref[0], 0.0) + jnp.where(m1, g1_ref[0], 0.0),
                               axis=-1, keepdims=True)


def _expert_kernel(be_ref, nv_ref, xs_ref, gs_ref, wg_ref, wu_ref, wd_ref, y_ref, acc_sc):
    j = pl.program_id(0)
    f = pl.program_id(1)
    valid = j < nv_ref[0]

    @pl.when(f == 0)
    def _():
        acc_sc[...] = jnp.zeros(acc_sc.shape, F32)

    @pl.when(valid)
    def _():
        x = xs_ref[...]
        hg = jnp.dot(x, wg_ref[...], preferred_element_type=F32)
        hu = jnp.dot(x, wu_ref[...], preferred_element_type=F32)
        mid = (jax.nn.silu(hg) * hu).astype(BF16)
        acc_sc[...] += jnp.dot(mid, wd_ref[...], preferred_element_type=F32)

    @pl.when(f == pl.num_programs(1) - 1)
    def _():
        y_ref[...] = jnp.where(valid, acc_sc[...] * gs_ref[...], 0.0).astype(y_ref.dtype)


def _combine_kernel(cj_ref, cs_ref, cfl_ref, x_ref, y_ref, p0_ref, p1_ref, o_ref, *, tg, ts):
    i = pl.program_id(0)
    fl = cfl_ref[i]

    @pl.when((fl & 2) != 0)
    def _():
        o_ref[...] = x_ref[...]

    @pl.when((fl & 1) != 0)
    def _():
        rio = lax.broadcasted_iota(I32, (tg, ts), 0) + cj_ref[i] * tg
        sel = jnp.where((p0_ref[0] == rio) | (p1_ref[0] == rio), 1.0, 0.0).astype(BF16)
        o_ref[...] += lax.dot_general(sel, y_ref[...], (((0,), (0,)), ((), ())), preferred_element_type=F32)


def _moe(x, hn, router_w, wg, wu, wd):
    t = x.shape[0]
    dffe = wg.shape[2]
    ts = min(1024, t)
    tg = min(256, t)
    te = min(1024, t)
    tf = 512
    assert t % ts == 0 and te % tg == 0 and dffe % tf == 0
    nb = t // ts
    n_eblk = (2 * t) // te + N_EXPERTS
    ns = n_eblk * te
    n_gblk = ns // tg
    n_items = n_gblk + N_EXPERTS * nb

    w_pad = jnp.zeros((D_MODEL, LANES), BF16).at[:, :N_EXPERTS].set(router_w.astype(BF16))
    tri = (jnp.arange(ts)[:, None] > jnp.arange(ts)[None, :]).astype(BF16)
    meta, cum, tot = pl.pallas_call(
        _router_kernel,
        grid=(nb,),
        in_specs=[pl.BlockSpec((ts, D_MODEL), lambda i: (i, 0)), pl.BlockSpec((D_MODEL, LANES), lambda i: (0, 0)),
                  pl.BlockSpec((ts, ts), lambda i: (0, 0))],
        out_specs=[pl.BlockSpec((1, ROUTER_FIELDS, ts), lambda i: (i, 0, 0)),
                   pl.BlockSpec((1, 1, LANES), lambda i: (i, 0, 0)), pl.BlockSpec((1, LANES), lambda i: (0, 0))],
        out_shape=[jax.ShapeDtypeStruct((nb, ROUTER_FIELDS, ts), F32), jax.ShapeDtypeStruct((nb, 1, LANES), F32),
                   jax.ShapeDtypeStruct((1, LANES), F32)],
        scratch_shapes=[pltpu.VMEM((1, LANES), F32)],
        compiler_params=_cparams("arbitrary"),
        name="moe_router",
    )(hn, w_pad, tri)

    e1 = meta[:, 0, :].astype(I32)
    e2 = meta[:, 1, :].astype(I32)
    g1 = meta[:, 2, :]
    g2 = meta[:, 3, :]
    r1 = meta[:, 4, :].astype(I32)
    r2 = meta[:, 5, :].astype(I32)
    counts = tot[0, :N_EXPERTS].astype(I32)
    cum_e = cum[:, 0, :N_EXPERTS].astype(I32)
    gpad = ((counts + te - 1) // te) * te
    gend = jnp.cumsum(gpad)
    gstart = gend - gpad

    def group_start(e):
        out = jnp.zeros(e.shape, I32)
        for ex in range(N_EXPERTS):
            out = jnp.where(e == ex, gstart[ex], out)
        return out

    pos0 = group_start(e1) + r1
    pos1 = group_start(e2) + r2
    nvalid_e = (gend[-1] // te).astype(I32)
    blk_e = jnp.arange(n_eblk, dtype=I32) * te
    be = jnp.minimum(jnp.sum(gend[None, :] <= blk_e[:, None], axis=1), N_EXPERTS - 1).astype(I32)
    be = jnp.where(jnp.arange(n_eblk) < nvalid_e, be, be[jnp.maximum(nvalid_e - 1, 0)])
    gb0 = jnp.arange(n_gblk, dtype=I32) * tg
    gb_valid = gb0 < gend[-1]
    gb_e = be[gb0 // te]
    rank0 = gb0 - gstart[gb_e]
    cnt_e = counts[gb_e]
    rank_last = jnp.maximum(jnp.minimum(rank0 + tg, cnt_e) - 1, 0)
    cum_cols = cum_e[:, gb_e]
    lo = jnp.sum(cum_cols <= jnp.minimum(rank0, rank_last)[None, :], axis=0) - 1
    hi = jnp.sum(cum_cols <= rank_last[None, :], axis=0) - 1
    n_it = jnp.where(gb_valid, hi - lo + 1, 0)
    it_end = jnp.cumsum(n_it)
    it_start = it_end - n_it
    total = it_end[-1]
    ii = jnp.arange(n_items, dtype=I32)
    live = ii < total
    iic = jnp.minimum(ii, total - 1)
    it_j = jnp.sum(it_end[None, :] <= iic[:, None], axis=1).astype(I32)
    it_s = (lo[it_j] + iic - it_start[it_j]).astype(I32)
    it_first = (iic == it_start[it_j]) & live
    it_fl = (live.astype(I32) + 2 * it_first.astype(I32)).astype(I32)
    order = jnp.argsort(jnp.where(live, it_s, nb), stable=True)
    cj = it_j[order]
    cs = it_s[order]
    cl = live[order]
    cs = jnp.where(cl, cs, cs[jnp.maximum(total - 1, 0)])
    cj = jnp.where(cl, cj, cj[jnp.maximum(total - 1, 0)])
    c_first = cl & jnp.concatenate([jnp.ones((1,), bool), cs[1:] != cs[:-1]])
    c_fl = (cl.astype(I32) + 2 * c_first.astype(I32)).astype(I32)

    row3 = lambda v: v.reshape(nb, 1, ts)

    gspec = pltpu.PrefetchScalarGridSpec(
        num_scalar_prefetch=3,
        grid=(n_items,),
        in_specs=[pl.BlockSpec((ts, D_MODEL), lambda i, ij, is_, fl: (is_[i], 0))]
        + [pl.BlockSpec((1, 1, ts), lambda i, ij, is_, fl: (is_[i], 0, 0)) for _ in range(4)],
        out_specs=[pl.BlockSpec((tg, D_MODEL), lambda i, ij, is_, fl: (ij[i], 0)),
                   pl.BlockSpec((tg, 1), lambda i, ij, is_, fl: (ij[i], 0))],
    )
    xs, gs = pl.pallas_call(
        functools.partial(_gather_kernel, tg=tg, ts=ts),
        grid_spec=gspec,
        out_shape=[jax.ShapeDtypeStruct((ns, D_MODEL), BF16), jax.ShapeDtypeStruct((ns, 1), F32)],
        compiler_params=_cparams("arbitrary"),
        name="moe_gather",
    )(it_j, it_s, it_fl, hn, row3(pos0), row3(pos1), row3(g1), row3(g2))

    espec = pltpu.PrefetchScalarGridSpec(
        num_scalar_prefetch=2,
        grid=(n_eblk, dffe // tf),
        in_specs=[pl.BlockSpec((te, D_MODEL), lambda j, f, be_, nv: (j, 0)),
                  pl.BlockSpec((te, 1), lambda j, f, be_, nv: (j, 0)),
                  pl.BlockSpec((None, D_MODEL, tf), lambda j, f, be_, nv: (be_[j], 0, f)),
                  pl.BlockSpec((None, D_MODEL, tf), lambda j, f, be_, nv: (be_[j], 0, f)),
                  pl.BlockSpec((None, tf, D_MODEL), lambda j, f, be_, nv: (be_[j], f, 0))],
        out_specs=pl.BlockSpec((te, D_MODEL), lambda j, f, be_, nv: (j, 0)),
        scratch_shapes=[pltpu.VMEM((te, D_MODEL), F32)],
    )
    y = pl.pallas_call(
        _expert_kernel,
        grid_spec=espec,
        out_shape=jax.ShapeDtypeStruct((ns, D_MODEL), BF16),
        compiler_params=_cparams("parallel", "arbitrary"),
        name="moe_experts",
    )(be, nvalid_e.reshape(1), xs, gs, wg, wu, wd)

    cspec = pltpu.PrefetchScalarGridSpec(
        num_scalar_prefetch=3,
        grid=(n_items,),
        in_specs=[pl.BlockSpec((ts, D_MODEL), lambda i, cj_, cs_, fl: (cs_[i], 0)),
                  pl.BlockSpec((tg, D_MODEL), lambda i, cj_, cs_, fl: (cj_[i], 0)),
                  pl.BlockSpec((1, 1, ts), lambda i, cj_, cs_, fl: (cs_[i], 0, 0)),
                  pl.BlockSpec((1, 1, ts), lambda i, cj_, cs_, fl: (cs_[i], 0, 0))],
        out_specs=pl.BlockSpec((ts, D_MODEL), lambda i, cj_, cs_, fl: (cs_[i], 0)),
    )
    return pl.pallas_call(
        functools.partial(_combine_kernel, tg=tg, ts=ts),
        grid_spec=cspec,
        out_shape=jax.ShapeDtypeStruct((t, D_MODEL), F32),
        compiler_params=_cparams("arbitrary"),
        name="moe_combine",
    )(cj, cs, c_fl, x, y, row3(pos0), row3(pos1))


def _layer(x, b, l, li, p, mem_k, mem_v, past, kv_stack=None):
    h = _rms(x, p["norm1"])
    w_in = p["w_in"]
    seg = lambda a: w_in[:, SPLITS[a]:(SPLITS[a + 1] if a + 1 < len(SPLITS) else None)]
    tile = lambda g, n: jnp.tile(g.astype(F32), n)
    (q,) = _mm(h, seg(0), epi="seg64", gain=tile(p["q_norm_a"], 2 * H_A) * (DH_A ** -0.5 * LOG2E),
               layouts=("head",), bl=(b, l))
    k_stack = None if kv_stack is None else (li, kv_stack[0], kv_stack[1])
    v_stack = None if kv_stack is None else (li, kv_stack[0], kv_stack[2])
    tok = "tok" if kv_stack is None else "stack"
    k32, kbf = _mm(h, seg(1), epi="seg64", gain=tile(p["k_norm_a"], 2 * H_A), out_dtypes=(F32, BF16),
                   layouts=(tok, "head"), bl=(b, l), stack=k_stack)
    v32, vbf = _mm(h, seg(2), out_dtypes=(F32, BF16), layouts=(tok, "head"), bl=(b, l), stack=v_stack)
    (u_tm,) = _mm(h, seg(3), out_dtypes=(F32,), layouts=("time",), bl=(b, l))
    (qm,) = _mm(h, _pad_heads(seg(4)), epi="pad192", gain=_pad_heads(tile(p["q_norm_m"], H_M) * (DH_M ** -0.5)))
    (gates,) = _mm(h, seg(5), epi="sigmoid", tn=1024)

    lam_init = 0.8 - 0.6 * math.exp(-0.3 * li)
    lam = (jnp.exp(jnp.sum(p["lam_q1"].astype(F32) * p["lam_k1"].astype(F32)))
           - jnp.exp(jnp.sum(p["lam_q2"].astype(F32) * p["lam_k2"].astype(F32))) + lam_init)
    if past is None:
        o_a = _attn_prompt(q, kbf, vbf, p["rel_bias"], p["subln_a"], lam, lam_init, b, l)
        zero = jnp.zeros((b, N_GROUPS, STATE_P), F32)
        y_b, s_re, s_im = _s5(u_tm, p, zero, zero, b, l)
    else:
        k_past, v_past, s0_re, s0_im = past
        o_a = _attn_sample(q, kbf, vbf, k_past, v_past, p["rel_bias"], p["subln_a"], lam, lam_init, b, l)
        y_b, s_re, s_im = _s5(u_tm, p, s0_re, s0_im, b, l)
    o_m = _memattn(qm, mem_k, mem_v, b, l)
    w_om_pad = _pad_heads(p["w_om"].T).T
    x, hn = _merge(x, o_a, y_b, o_m, gates, p["w_oa"], p["w_ob"], w_om_pad, p["w_out"], p["norm2"], b, l)
    if li % 2 == 0:
        x = _ffn(x, hn, p["ffn_w_gate"], p["ffn_w_up"], p["ffn_w_down"])
    else:
        x = _moe(x, hn, p["router"], p["moe_w_gate"], p["moe_w_up"], p["moe_w_down"])
    return x, k32, v32, s_re, s_im


def kernel(x_prompt, x_sample, mem_prompt, cache_attn_k, cache_attn_v, state_ssm_re, state_ssm_im, cache_mem_k, cache_mem_v, norm1, norm2, w_in, q_norm_a, k_norm_a, lam_q1, lam_k1, lam_q2, lam_k2, subln_a, w_oa, lambda_re, lambda_im, log_dt, b_re, b_im, c_re, c_im, d_skip, w_glu, b_glu, w_ob, w_mk, w_mv, q_norm_m, k_norm_m, w_om, w_out, rel_bias, ffn_w_gate, ffn_w_up, ffn_w_down, router, moe_w_gate, moe_w_up, moe_w_down):
    depth = w_in.shape[0]
    bp, lp, _ = x_prompt.shape
    bs, ls, _ = x_sample.shape
    n_mem = mem_prompt.shape[1]
    xp = x_prompt.reshape(bp * lp, D_MODEL)
    xs = x_sample.reshape(bs * ls, D_MODEL)
    mem_bf = mem_prompt.reshape(bp * n_mem, D_MODEL).astype(BF16)
    outs = {name: [] for name in ("srp", "sip", "mkp", "mvp", "ks", "vs", "srs", "sis")}
    kp_buf = vp_buf = None
    for li in range(depth):
        j = li // 2
        p = {
            "norm1": norm1[li], "norm2": norm2[li], "w_in": w_in[li].astype(BF16),
            "q_norm_a": q_norm_a[li], "k_norm_a": k_norm_a[li],
            "lam_q1": lam_q1[li], "lam_k1": lam_k1[li], "lam_q2": lam_q2[li], "lam_k2": lam_k2[li],
            "subln_a": subln_a[li], "w_oa": w_oa[li].astype(BF16),
            "lambda_re": lambda_re[li], "lambda_im": lambda_im[li], "log_dt": log_dt[li],
            "b_re": b_re[li], "b_im": b_im[li], "c_re": c_re[li], "c_im": c_im[li],
            "d_skip": d_skip[li], "w_glu": w_glu[li], "b_glu": b_glu[li], "w_ob": w_ob[li].astype(BF16),
            "q_norm_m": q_norm_m[li], "w_om": w_om[li].astype(BF16), "w_out": w_out[li].astype(BF16),
            "rel_bias": rel_bias,
        }
        if li % 2 == 0:
            p.update(ffn_w_gate=ffn_w_gate[j].astype(BF16), ffn_w_up=ffn_w_up[j].astype(BF16),
                     ffn_w_down=ffn_w_down[j].astype(BF16))
        else:
            p.update(router=router[j], moe_w_gate=moe_w_gate[j].astype(BF16), moe_w_up=moe_w_up[j].astype(BF16),
                     moe_w_down=moe_w_down[j].astype(BF16))
        mk32, mkbf = _mm(mem_bf, w_mk[li].astype(BF16), epi="seg192", gain=jnp.tile(k_norm_m[li].astype(F32), H_M),
                         out_dtypes=(F32, BF16))
        mv32, mvbf = _mm(mem_bf, w_mv[li].astype(BF16), out_dtypes=(F32, BF16))
        xp, kp_buf, vp_buf, sr_p, si_p = _layer(xp, bp, lp, li, p, _pad_heads(mkbf.reshape(bp, n_mem, W_M)),
                                                _pad_heads(mvbf.reshape(bp, n_mem, W_M)), None,
                                                kv_stack=(depth, kp_buf, vp_buf))
        past = (cache_attn_k[li], cache_attn_v[li], state_ssm_re[li], state_ssm_im[li])
        xs, k_s, v_s, sr_s, si_s = _layer(xs, bs, ls, li, p,
                                          _pad_heads(cache_mem_k[li].reshape(bs, n_mem, W_M).astype(BF16)),
                                          _pad_heads(cache_mem_v[li].reshape(bs, n_mem, W_M).astype(BF16)), past)
        outs["srp"].append(sr_p)
        outs["sip"].append(si_p)
        outs["mkp"].append(mk32.reshape(bp, n_mem, H_M, DH_M))
        outs["mvp"].append(mv32.reshape(bp, n_mem, H_M, DH_M))
        outs["ks"].append(k_s.reshape(bs, ls, H_A, DV_A))
        outs["vs"].append(v_s.reshape(bs, ls, H_A, DV_A))
        outs["srs"].append(sr_s)
        outs["sis"].append(si_s)
    st = lambda name: jnp.stack(outs[name])
    kv_shape = (depth, bp, lp, H_A, DV_A)
    return (xp.reshape(bp, lp, D_MODEL), xs.reshape(bs, ls, D_MODEL),
            kp_buf.reshape(kv_shape), vp_buf.reshape(kv_shape), st("srp"), st("sip"),
            st("mkp"), st("mvp"), st("ks"), st("vs"), st("srs"), st("sis"))
```

```python
import functools
import math

import jax
import jax.numpy as jnp
from jax import lax
from jax.experimental import pallas as pl
from jax.experimental.pallas import tpu as pltpu

F32 = jnp.float32
BF16 = jnp.bfloat16
I32 = jnp.int32

D_MODEL = 1024
CHUNK = 64
H_A = 8
DH_A = 64
DV_A = 2 * DH_A
W_A = H_A * DV_A
GROUP_CH = 16
N_GROUPS = 48
STATE_P = 64
W_B = N_GROUPS * GROUP_CH
H_M = 4
DH_M = 192
W_M = H_M * DH_M
DH_M_PAD = 256
W_M_PAD = H_M * DH_M_PAD
REL_BUCKETS = 32
REL_MAX_DIST = 128
N_EXPERTS = 8
ROUTER_FIELDS = 8
EPS = 1e-6
NEG_INF = -1e30
LOG2E = math.log2(math.e)
Q_A_W = H_A * 2 * DH_A
SPLITS = (0, Q_A_W, 2 * Q_A_W, 2 * Q_A_W + W_A, 2 * Q_A_W + W_A + W_B, 2 * Q_A_W + W_A + W_B + W_M)
GATE_W = 3 * D_MODEL

LANES = 128
VMEM_LIMIT = 56 * 1024 * 1024
ATTN_GROUP = 4
S5_SLAB_GROUPS = LANES // GROUP_CH
S5_SLABS = N_GROUPS // S5_SLAB_GROUPS
S5_SLAB_STATE = S5_SLAB_GROUPS * STATE_P


def _cparams(*sem):
    return pltpu.CompilerParams(dimension_semantics=sem, vmem_limit_bytes=VMEM_LIMIT)


def _rms_kernel(x_ref, g_ref, o_ref):
    x = x_ref[...]
    ms = jnp.mean(x * x, axis=-1, keepdims=True)
    o_ref[...] = ((x * lax.rsqrt(ms + EPS)) * g_ref[...]).astype(o_ref.dtype)


def _rms(x, g):
    t, d = x.shape
    tm = min(1024, t)
    return pl.pallas_call(
        _rms_kernel,
        grid=(t // tm,),
        in_specs=[pl.BlockSpec((tm, d), lambda i: (i, 0)), pl.BlockSpec((1, d), lambda i: (0, 0))],
        out_specs=pl.BlockSpec((tm, d), lambda i: (i, 0)),
        out_shape=jax.ShapeDtypeStruct((t, d), BF16),
        compiler_params=_cparams("parallel"),
        name="rmsnorm",
    )(x, g.reshape(1, d).astype(F32))


def _seg64_scale(ys):
    lane = lax.broadcasted_iota(I32, (1, LANES), 1)
    left = lane < DH_A
    y2 = ys * ys
    sl = jnp.sum(jnp.where(left, y2, 0.0), axis=-1, keepdims=True)
    sr = jnp.sum(jnp.where(left, 0.0, y2), axis=-1, keepdims=True)
    rl = lax.rsqrt(sl * (1.0 / DH_A) + EPS)
    rr = lax.rsqrt(sr * (1.0 / DH_A) + EPS)
    return jnp.where(left, rl, rr)


def _seg192_scale(y):
    col = lax.broadcasted_iota(I32, (1, W_M), 1)
    y2 = y * y
    rb = jnp.zeros_like(y)
    for h in range(H_M):
        m = (col >= DH_M * h) & (col < DH_M * (h + 1))
        s = jnp.sum(jnp.where(m, y2, 0.0), axis=-1, keepdims=True)
        rb = jnp.where(m, lax.rsqrt(s / DH_M + EPS), rb)
    return rb


def _mm_kernel(*refs, epi, layouts, has_gain):
    a_ref, w_ref = refs[0], refs[1]
    g_ref = refs[2] if has_gain else None
    outs = refs[len(refs) - len(layouts):]
    y = jnp.dot(a_ref[...], w_ref[...], preferred_element_type=F32)
    tn = y.shape[1]
    if epi == "seg192":
        y = (y * _seg192_scale(y)) * g_ref[...]
    elif epi == "pad192":
        for h in range(tn // DH_M_PAD):
            sl = slice(h * DH_M_PAD, (h + 1) * DH_M_PAD)
            ys = y[:, sl]
            r = lax.rsqrt(jnp.sum(ys * ys, axis=-1, keepdims=True) / DH_M + EPS)
            for o in outs:
                o[:, sl] = ((ys * r) * g_ref[:, sl]).astype(o.dtype)
        return
    elif epi == "sigmoid":
        y = jax.nn.sigmoid(y)
    if epi == "seg64" or "head" in layouts:
        for s in range(tn // LANES):
            sl = slice(s * LANES, (s + 1) * LANES)
            ys = y[:, sl]
            if epi == "seg64":
                ys = (ys * _seg64_scale(ys)) * g_ref[:, sl]
            for o, lay in zip(outs, layouts):
                if lay == "head":
                    o[s] = ys.astype(o.dtype)
                else:
                    o[:, sl] = ys.astype(o.dtype)
        return
    for o in outs:
        o[...] = y.astype(o.dtype)


def _mm(a, w, *, epi="plain", gain=None, out_dtypes=(BF16,), layouts=None, tn=None, bl=None, stack=None):
    t, k = a.shape
    n = w.shape[1]
    tn = n if tn is None else tn
    layouts = ("tok",) * len(out_dtypes) if layouts is None else layouts
    assert n % tn == 0 and tn % LANES == 0
    if bl is None:
        assert all(lay in ("tok", "stack") for lay in layouts)
        tm, nl = min(1024, t), 1
    else:
        b, l = bl
        tm = min(1024, l)
        assert l % tm == 0
        nl = l // tm
    assert t % tm == 0
    out_specs, out_shapes = [], []
    alias_args, aliases = [], {}
    n_in = 2 + int(gain is not None)
    for oi, (dt, lay) in enumerate(zip(out_dtypes, layouts)):
        if lay == "tok":
            out_specs.append(pl.BlockSpec((tm, tn), lambda i, j: (i, j)))
            out_shapes.append(jax.ShapeDtypeStruct((t, n), dt))
        elif lay == "stack":
            li, depth, buf = stack
            out_specs.append(pl.BlockSpec((None, tm, tn), lambda i, j: (li, i, j)))
            out_shapes.append(jax.ShapeDtypeStruct((depth, t, n), dt))
            if buf is not None:
                aliases[n_in + len(alias_args)] = oi
                alias_args.append(buf)
        elif lay == "time":
            assert tn == n
            out_specs.append(pl.BlockSpec((tm, tn), lambda i, j: (i % nl, i // nl)))
            out_shapes.append(jax.ShapeDtypeStruct((l, b * n), dt))
        else:
            out_specs.append(pl.BlockSpec((None, tn // LANES, tm, LANES), lambda i, j: (i // nl, j, i % nl, 0)))
            out_shapes.append(jax.ShapeDtypeStruct((b, n // LANES, l, LANES), dt))
    in_specs = [pl.BlockSpec((tm, k), lambda i, j: (i, 0)), pl.BlockSpec((k, tn), lambda i, j: (0, j))]
    args = [a, w]
    if gain is not None:
        in_specs.append(pl.BlockSpec((1, tn), lambda i, j: (0, j)))
        args.append(gain.reshape(1, n).astype(F32))
    in_specs += [pl.BlockSpec(memory_space=pl.ANY) for _ in alias_args]
    return pl.pallas_call(
        functools.partial(_mm_kernel, epi=epi, layouts=tuple(layouts), has_gain=gain is not None),
        grid=(t // tm, n // tn),
        in_specs=in_specs,
        out_specs=out_specs,
        out_shape=out_shapes,
        input_output_aliases=aliases,
        compiler_params=_cparams("parallel", "parallel"),
        name="mm_" + epi,
    )(*args, *alias_args)


def _rel_bias(q_pos, k_pos, table):
    rel = k_pos[None, :] - q_pos[:, None]
    half = REL_BUCKETS // 2
    max_exact = half // 2
    n = jnp.abs(rel)
    nf = jnp.maximum(n, 1).astype(F32)
    large = max_exact + (jnp.log(nf / max_exact) / math.log(REL_MAX_DIST / max_exact)
                         * (half - max_exact)).astype(I32)
    large = jnp.minimum(large, half - 1)
    bucket = jnp.where(rel > 0, half, 0) + jnp.where(n < max_exact, n, large)
    tab = table.astype(F32)
    bias = jnp.zeros((H_A,) + bucket.shape, F32)
    for bk in range(REL_BUCKETS):
        bias = jnp.where((bucket == bk)[None], tab[bk][:, None, None], bias)
    visible = (k_pos[None, :] // CHUNK) <= (q_pos[:, None] // CHUNK)
    return jnp.where(visible[None], bias, NEG_INF)


def _split_halves(q):
    lane = lax.broadcasted_iota(I32, (1, LANES), 1)
    zero = jnp.zeros_like(q)
    return jnp.concatenate([jnp.where(lane < DH_A, q, zero), jnp.where(lane >= DH_A, q, zero)], axis=0)


def _subln(o, g, lam_init):
    ms = jnp.mean(o * o, axis=-1, keepdims=True)
    return ((o * lax.rsqrt(ms + EPS)) * g) * (1.0 - lam_init)


def _attn_prompt_kernel(lam_ref, q_ref, k_ref, v_ref, bias_ref, g_ref, o_ref,
                        s_sc, m_sc, acc_sc, *, tq, lam_init):
    qi = pl.program_id(2)
    n_blk = qi + 1
    qs = _split_halves(q_ref[...])
    nt = (((1,), (1,)), ((), ()))
    nc = tq // LANES

    def fold(op, s):
        r = s[:, :LANES]
        for c in range(1, nc):
            r = op(r, s[:, c * LANES:(c + 1) * LANES])
        return r

    def score(j):
        r0 = pl.multiple_of(j * tq, tq)
        bias = bias_ref[jnp.clip(j - (qi - 2), 0, 2)]
        s = lax.dot_general(qs, k_ref[pl.ds(r0, tq), :], nt, preferred_element_type=F32)
        s = s + jnp.concatenate([bias, bias], axis=0)
        s_sc[j] = s
        return fold(jnp.maximum, s)

    def sweep(group_fn):
        n_full = n_blk // ATTN_GROUP

        def trip(g, c):
            group_fn(ATTN_GROUP * g, ATTN_GROUP)
            return c

        lax.fori_loop(0, n_full, trip, 0)
        rem = n_blk - ATTN_GROUP * n_full

        @pl.when(rem >= 2)
        def _():
            group_fn(ATTN_GROUP * n_full, 2)

        @pl.when(rem % 2 == 1)
        def _():
            group_fn(qi, 1)

    m_sc[...] = jnp.full(m_sc.shape, -jnp.inf, F32)

    def score_group(j0, g):
        m_sc[...] = jnp.maximum(m_sc[...], functools.reduce(jnp.maximum, [score(j0 + t) for t in range(g)]))

    sweep(score_group)
    m = jnp.max(m_sc[...], axis=-1, keepdims=True)
    m_sc[...] = jnp.broadcast_to(m, m_sc.shape)
    acc_sc[...] = jnp.zeros(acc_sc.shape, F32)
    ones = jnp.ones((tq, DV_A), BF16)

    def pv(j):
        mrow = m_sc[...]
        s = s_sc[j]
        p = jnp.concatenate([jnp.exp2(s[:, cc * LANES:(cc + 1) * LANES] - mrow) for cc in range(nc)], axis=1)
        r0 = pl.multiple_of(j * tq, tq)
        v_ext = jnp.concatenate([v_ref[pl.ds(r0, tq), :], ones], axis=1)
        return jnp.dot(p.astype(BF16), v_ext, preferred_element_type=F32)

    def pv_group(j0, g):
        acc_sc[...] += functools.reduce(jnp.add, [pv(j0 + t) for t in range(g)])

    sweep(pv_group)
    acc = acc_sc[...]
    o = acc[:, :DV_A] / acc[:, DV_A:]
    o = o[:tq] - lam_ref[0] * o[tq:]
    o_ref[...] = _subln(o, g_ref[...], lam_init).astype(o_ref.dtype)


def _attn_prompt(q, k, v, rel_table, subln_g, lam, lam_init, b, l):
    tq = min(512, l)
    assert l % tq == 0 and tq % CHUNK == 0 and tq >= 128
    cfar = _rel_bias(jnp.array([2 * tq + 1]), jnp.array([0]), rel_table)[:, 0, 0]
    bias = _rel_bias(jnp.arange(tq, 2 * tq), jnp.arange(2 * tq), rel_table)
    bias = (bias - cfar[:, None, None]) * LOG2E
    bias = jnp.stack([jnp.zeros_like(bias[:, :, :tq]), bias[:, :, :tq], bias[:, :, tq:]], axis=1)
    smem = pl.BlockSpec(memory_space=pltpu.SMEM)
    out = pl.pallas_call(
        functools.partial(_attn_prompt_kernel, tq=tq, lam_init=lam_init),
        grid=(b, H_A, l // tq),
        in_specs=[smem,
                  pl.BlockSpec((None, None, tq, DV_A), lambda bi, h, i: (bi, h, i, 0)),
                  pl.BlockSpec((None, None, l, DV_A), lambda bi, h, i: (bi, h, 0, 0)),
                  pl.BlockSpec((None, None, l, DV_A), lambda bi, h, i: (bi, h, 0, 0)),
                  pl.BlockSpec((None, 3, tq, tq), lambda bi, h, i: (h, 0, 0, 0)),
                  pl.BlockSpec((1, DV_A), lambda bi, h, i: (0, 0))],
        out_specs=pl.BlockSpec((None, tq, DV_A), lambda bi, h, i: (bi, i, h)),
        out_shape=jax.ShapeDtypeStruct((b, l, W_A), BF16),
        scratch_shapes=[pltpu.VMEM((l // tq, 2 * tq, tq), F32), pltpu.VMEM((2 * tq, LANES), F32),
                        pltpu.VMEM((2 * tq, 2 * DV_A), F32)],
        compiler_params=_cparams("parallel", "parallel", "arbitrary"),
        name="attn_prompt",
    )(lam.reshape(1), q, k, v, bias, subln_g.reshape(1, DV_A).astype(F32))
    return out.reshape(b * l, W_A)


def _attn_sample_kernel(lam_ref, q_ref, kp_ref, vp_ref, kn_ref, vn_ref, bp_ref, bn_ref, g_ref, o_ref,
                        *, lq, lam_init):
    lam = lam_ref[0]
    nt = (((1,), (1,)), ((), ()))
    for h in range(H_A):
        sl = slice(h * DV_A, (h + 1) * DV_A)
        qs = _split_halves(q_ref[h])
        kp = kp_ref[:, h, :].astype(BF16)
        vp = vp_ref[:, h, :].astype(BF16)
        bp = bp_ref[h]
        bn = bn_ref[h]
        sp = lax.dot_general(qs, kp, nt, preferred_element_type=F32) + jnp.concatenate([bp, bp], axis=0)
        sn = lax.dot_general(qs, kn_ref[h], nt, preferred_element_type=F32) + jnp.concatenate([bn, bn], axis=0)
        m = jnp.maximum(jnp.max(sp, axis=-1, keepdims=True), jnp.max(sn, axis=-1, keepdims=True))
        pp = jnp.exp2(sp - m)
        pn = jnp.exp2(sn - m)
        lsum = jnp.sum(pp, axis=-1, keepdims=True) + jnp.sum(pn, axis=-1, keepdims=True)
        o = (jnp.dot(pp.astype(BF16), vp, preferred_element_type=F32)
             + jnp.dot(pn.astype(BF16), vn_ref[h], preferred_element_type=F32)) / lsum
        o = o[:lq] - lam * o[lq:]
        o_ref[:, sl] = _subln(o, g_ref[...], lam_init).astype(o_ref.dtype)


def _attn_sample(q, k_new, v_new, cache_k, cache_v, li, rel_table, subln_g, lam, lam_init, b, l):
    past = cache_k.shape[2]
    k_pos = jnp.arange(past + l)
    bias = _rel_bias(k_pos[past:], k_pos, rel_table) * LOG2E
    bp, bn = bias[:, :, :past], bias[:, :, past:]
    smem = pl.BlockSpec(memory_space=pltpu.SMEM)
    tok = lambda rows: pl.BlockSpec((None, rows, W_A), lambda bi: (bi, 0, 0))
    head = pl.BlockSpec((None, H_A, l, DV_A), lambda bi: (bi, 0, 0, 0))
    cache = pl.BlockSpec((None, None, past, H_A, DV_A), lambda bi: (li, bi, 0, 0, 0))
    full = lambda shape: pl.BlockSpec(shape, lambda bi: tuple(0 for _ in shape))
    out = pl.pallas_call(
        functools.partial(_attn_sample_kernel, lq=l, lam_init=lam_init),
        grid=(b,),
        in_specs=[smem, head, cache, cache, head, head,
                  full((H_A, l, past)), full((H_A, l, l)), full((1, DV_A))],
        out_specs=tok(l),
        out_shape=jax.ShapeDtypeStruct((b, l, W_A), BF16),
        compiler_params=_cparams("parallel"),
        name="attn_sample",
    )(lam.reshape(1), q, cache_k, cache_v, k_new, v_new, bp, bn, subln_g.reshape(1, DV_A).astype(F32))
    return out.reshape(b * l, W_A)


def _pad_heads(z):
    lead = z.shape[:-1]
    z = z.reshape(*lead, H_M, DH_M)
    z = jnp.pad(z, [(0, 0)] * len(lead) + [(0, 0), (0, DH_M_PAD - DH_M)])
    return z.reshape(*lead, W_M_PAD)


def _memattn_kernel(q_ref, mk_ref, mv_ref, o_ref):
    nt = (((1,), (1,)), ((), ()))
    for h in range(H_M):
        sl = slice(h * DH_M_PAD, (h + 1) * DH_M_PAD)
        s = lax.dot_general(q_ref[:, sl], mk_ref[:, sl], nt, preferred_element_type=F32)
        m = jnp.max(s, axis=-1, keepdims=True)
        p = jnp.exp(s - m)
        p = p / jnp.sum(p, axis=-1, keepdims=True)
        o_ref[:, sl] = jnp.dot(p.astype(BF16), mv_ref[:, sl], preferred_element_type=F32).astype(o_ref.dtype)


def _memattn(qm, mk, mv, b, l):
    n_mem = mk.shape[1]
    tq = min(512, l)
    assert l % tq == 0
    out = pl.pallas_call(
        _memattn_kernel,
        grid=(b, l // tq),
        in_specs=[pl.BlockSpec((None, tq, W_M_PAD), lambda bi, i: (bi, i, 0)),
                  pl.BlockSpec((None, n_mem, W_M_PAD), lambda bi, i: (bi, 0, 0)),
                  pl.BlockSpec((None, n_mem, W_M_PAD), lambda bi, i: (bi, 0, 0))],
        out_specs=pl.BlockSpec((None, tq, W_M_PAD), lambda bi, i: (bi, i, 0)),
        out_shape=jax.ShapeDtypeStruct((b, l, W_M_PAD), BF16),
        compiler_params=_cparams("parallel", "parallel"),
        name="memattn",
    )(qm.reshape(b, l, W_M_PAD), mk, mv)
    return out.reshape(b * l, W_M_PAD)


def _s5_tables(p):
    dt = jnp.exp(p["log_dt"].astype(F32))[:, None]
    lr = jnp.minimum(p["lambda_re"].astype(F32), -1e-4)
    lim = p["lambda_im"].astype(F32)
    mag = jnp.exp(lr * dt)
    ar = mag * jnp.cos(lim * dt)
    ai = mag * jnp.sin(lim * dt)
    den = lr * lr + lim * lim
    fr = ((ar - 1.0) * lr + ai * lim) / den
    fi = (ai * lr - (ar - 1.0) * lim) / den
    br = p["b_re"].astype(F32)
    bi = p["b_im"].astype(F32)
    bbr = fr[..., None] * br - fi[..., None] * bi
    bbi = fr[..., None] * bi + fi[..., None] * br
    eye = jnp.eye(S5_SLAB_GROUPS, dtype=F32)
    sg = (S5_SLABS, S5_SLAB_GROUPS)

    def in_w(bb):
        w = jnp.einsum("sgpc,gh->sgchp", bb.reshape(*sg, STATE_P, GROUP_CH), eye)
        return w.reshape(S5_SLABS, LANES, S5_SLAB_STATE)

    def out_w(c):
        w = jnp.einsum("sgcp,gh->sgphc", c.reshape(*sg, GROUP_CH, STATE_P), eye)
        return w.reshape(S5_SLABS, S5_SLAB_STATE, LANES)

    wb = jnp.concatenate([in_w(bbr), in_w(bbi)], axis=2).astype(BF16)
    wc = jnp.concatenate([out_w(p["c_re"].astype(F32)), -out_w(p["c_im"].astype(F32))], axis=1).astype(BF16)
    a_re = ar.reshape(S5_SLABS, 1, S5_SLAB_STATE)
    a_im = ai.reshape(S5_SLABS, 1, S5_SLAB_STATE)
    return wb, wc, a_re, a_im


def _s5_kernel(u_ref, wb_ref, wc_ref, ar_ref, ai_ref, d_ref, wglu_ref, bglu_ref, s0r_ref, s0i_ref,
               y_ref, sr_ref, si_ref, bu_sc, x_sc, y_sc, *, tl, nb):
    i = pl.program_id(0)

    @pl.when(i == 0)
    def _():
        sr_ref[...] = s0r_ref[...]
        si_ref[...] = s0i_ref[...]

    rows = tl * nb
    u = u_ref[...].reshape(rows, W_B)
    y_sc[...] = d_ref[...] * u
    for s in range(S5_SLABS):
        sl = slice(s * LANES, (s + 1) * LANES)
        bu_sc[...] = jnp.dot(u[:, sl].astype(BF16), wb_ref[s], preferred_element_type=F32)
        ar = jnp.broadcast_to(ar_ref[s], (nb, S5_SLAB_STATE))
        ai = jnp.broadcast_to(ai_ref[s], (nb, S5_SLAB_STATE))

        def body(t, carry):
            sr, si = carry
            r0 = pl.multiple_of(t * nb, nb)
            b_r = bu_sc[pl.ds(r0, nb), :S5_SLAB_STATE]
            b_i = bu_sc[pl.ds(r0, nb), S5_SLAB_STATE:]
            nr = ar * sr - ai * si + b_r
            ni = ar * si + ai * sr + b_i
            x_sc[pl.ds(r0, nb), :S5_SLAB_STATE] = nr.astype(BF16)
            x_sc[pl.ds(r0, nb), S5_SLAB_STATE:] = ni.astype(BF16)
            return nr, ni

        sr, si = lax.fori_loop(0, tl, body, (sr_ref[s], si_ref[s]), unroll=True)
        sr_ref[s] = sr
        si_ref[s] = si
        y_sc[:, sl] = y_sc[:, sl] + jnp.dot(x_sc[...], wc_ref[s], preferred_element_type=F32)
    yg = jax.nn.gelu(y_sc[...])
    z = jnp.dot(yg.astype(BF16), wglu_ref[...], preferred_element_type=F32) + bglu_ref[...]
    out = yg * jax.nn.sigmoid(z)
    y_ref[...] = out.astype(y_ref.dtype).reshape(tl, nb, W_B)


def _s5(u_tm, p, s0_re, s0_im, b, l):
    assert b % 16 == 0
    tl = min(64, l)
    assert l % tl == 0
    wb, wc, a_re, a_im = _s5_tables(p)
    to_slab = lambda s0: jnp.transpose(s0.astype(F32).reshape(b, S5_SLABS, S5_SLAB_STATE), (1, 0, 2))
    from_slab = lambda st: jnp.transpose(st, (1, 0, 2)).reshape(b, N_GROUPS, STATE_P)
    full = lambda shape: pl.BlockSpec(shape, lambda i: tuple(0 for _ in shape))
    st_shape = (S5_SLABS, b, S5_SLAB_STATE)
    rows = tl * b
    y, sr, si = pl.pallas_call(
        functools.partial(_s5_kernel, tl=tl, nb=b),
        grid=(l // tl,),
        in_specs=[pl.BlockSpec((tl, b, W_B), lambda i: (i, 0, 0)),
                  full(wb.shape), full(wc.shape), full(a_re.shape), full(a_im.shape),
                  full((1, W_B)), full((W_B, W_B)), full((1, W_B)), full(st_shape), full(st_shape)],
        out_specs=[pl.BlockSpec((tl, b, W_B), lambda i: (i, 0, 0)), full(st_shape), full(st_shape)],
        out_shape=[jax.ShapeDtypeStruct((l, b, W_B), BF16),
                   jax.ShapeDtypeStruct(st_shape, F32), jax.ShapeDtypeStruct(st_shape, F32)],
        scratch_shapes=[pltpu.VMEM((rows, 2 * S5_SLAB_STATE), F32),
                        pltpu.VMEM((rows, 2 * S5_SLAB_STATE), BF16),
                        pltpu.VMEM((rows, W_B), F32)],
        compiler_params=_cparams("arbitrary"),
        name="s5_scan",
    )(u_tm.reshape(l, b, W_B), wb, wc, a_re, a_im,
      p["d_skip"].astype(F32).reshape(1, W_B), p["w_glu"].astype(BF16), p["b_glu"].astype(F32).reshape(1, W_B),
      to_slab(s0_re), to_slab(s0_im))
    return y.reshape(l, b * W_B), from_slab(sr), from_slab(si)


def _merge_kernel(x_ref, oa_ref, yb_ref, om_ref, gt_ref, woa_ref, wob_ref, wom_ref, wout_ref, g2_ref,
                  xo_ref, hn_ref):
    pa = jnp.dot(oa_ref[...], woa_ref[...], preferred_element_type=F32)
    pb = jnp.dot(yb_ref[...], wob_ref[...], preferred_element_type=F32)
    pm = jnp.dot(om_ref[...], wom_ref[...], preferred_element_type=F32)
    d = D_MODEL
    mix = (gt_ref[:, :d].astype(F32) * pa + gt_ref[:, d:2 * d].astype(F32) * pb
           + gt_ref[:, 2 * d:].astype(F32) * pm)
    x = x_ref[...] + jnp.dot(mix.astype(BF16), wout_ref[...], preferred_element_type=F32)
    xo_ref[...] = x
    ms = jnp.mean(x * x, axis=-1, keepdims=True)
    hn_ref[...] = ((x * lax.rsqrt(ms + EPS)) * g2_ref[...]).astype(hn_ref.dtype)


def _merge(x, o_a, y_b_tm, o_m, gates, w_oa, w_ob, w_om, w_out, g2, b, l):
    t = b * l
    tm = min(512, l)
    assert l % tm == 0
    nl = l // tm
    row = lambda w: pl.BlockSpec((tm, w), lambda bi, i: (bi * nl + i, 0))
    full = lambda shape: pl.BlockSpec(shape, lambda bi, i: (0, 0))
    return pl.pallas_call(
        _merge_kernel,
        grid=(b, nl),
        in_specs=[row(D_MODEL), row(W_A), pl.BlockSpec((tm, W_B), lambda bi, i: (i, bi)), row(W_M_PAD), row(GATE_W),
                  full(w_oa.shape), full(w_ob.shape), full(w_om.shape), full(w_out.shape), full((1, D_MODEL))],
        out_specs=[row(D_MODEL), row(D_MODEL)],
        out_shape=[jax.ShapeDtypeStruct((t, D_MODEL), F32), jax.ShapeDtypeStruct((t, D_MODEL), BF16)],
        compiler_params=_cparams("parallel", "parallel"),
        name="merge",
    )(x, o_a, y_b_tm, o_m, gates, w_oa, w_ob, w_om, w_out, g2.reshape(1, D_MODEL).astype(F32))


def _ffn_kernel(x_ref, h_ref, wg_ref, wu_ref, wd_ref, o_ref, acc_sc):
    f = pl.program_id(1)

    @pl.when(f == 0)
    def _():
        acc_sc[...] = x_ref[...]

    h = h_ref[...]
    hg = jnp.dot(h, wg_ref[...], preferred_element_type=F32)
    hu = jnp.dot(h, wu_ref[...], preferred_element_type=F32)
    mid = (jax.nn.silu(hg) * hu).astype(BF16)
    acc_sc[...] += jnp.dot(mid, wd_ref[...], preferred_element_type=F32)

    @pl.when(f == pl.num_programs(1) - 1)
    def _():
        o_ref[...] = acc_sc[...]


def _ffn(x, hn, wg, wu, wd):
    t = x.shape[0]
    dff = wg.shape[1]
    tm = min(512, t)
    tf = dff // 2 if (dff // 2) % LANES == 0 else dff
    return pl.pallas_call(
        _ffn_kernel,
        grid=(t // tm, dff // tf),
        in_specs=[pl.BlockSpec((tm, D_MODEL), lambda i, f: (i, 0)), pl.BlockSpec((tm, D_MODEL), lambda i, f: (i, 0)),
                  pl.BlockSpec((D_MODEL, tf), lambda i, f: (0, f)), pl.BlockSpec((D_MODEL, tf), lambda i, f: (0, f)),
                  pl.BlockSpec((tf, D_MODEL), lambda i, f: (f, 0))],
        out_specs=pl.BlockSpec((tm, D_MODEL), lambda i, f: (i, 0)),
        out_shape=jax.ShapeDtypeStruct((t, D_MODEL), F32),
        scratch_shapes=[pltpu.VMEM((tm, D_MODEL), F32)],
        compiler_params=_cparams("parallel", "arbitrary"),
        name="ffn_dense",
    )(x, hn, wg, wu, wd)


def _router_kernel(h_ref, w_ref, tri_ref, meta_ref, cum_ref, tot_ref, carry_sc):
    i = pl.program_id(0)

    @pl.when(i == 0)
    def _():
        carry_sc[...] = jnp.zeros(carry_sc.shape, F32)

    lane = lax.broadcasted_iota(I32, (1, LANES), 1)
    logits = jnp.dot(h_ref[...], w_ref[...], preferred_element_type=F32)
    lg = jnp.where(lane < N_EXPERTS, logits, -jnp.inf)
    m1 = jnp.max(lg, axis=-1, keepdims=True)
    i1 = jnp.min(jnp.where(lg == m1, lane, LANES), axis=-1, keepdims=True)
    lg2 = jnp.where(lane == i1, -jnp.inf, lg)
    m2 = jnp.max(lg2, axis=-1, keepdims=True)
    i2 = jnp.min(jnp.where(lg2 == m2, lane, LANES), axis=-1, keepdims=True)
    e = jnp.exp(m2 - m1)
    g1 = 1.0 / (1.0 + e)
    g2 = e / (1.0 + e)
    hit1 = lane == i1
    hit2 = lane == i2
    cnt = jnp.where(hit1 | hit2, 1.0, 0.0)
    carry = carry_sc[...]
    before = jnp.dot(tri_ref[...], cnt.astype(BF16), preferred_element_type=F32) + carry
    r1 = jnp.sum(jnp.where(hit1, before, 0.0), axis=-1, keepdims=True)
    r2 = jnp.sum(jnp.where(hit2, before, 0.0), axis=-1, keepdims=True)
    cum_ref[0] = carry
    carry = carry + jnp.sum(cnt, axis=0, keepdims=True)
    carry_sc[...] = carry
    tot_ref[...] = carry
    meta = jnp.zeros(logits.shape, F32)
    for c, val in enumerate((i1.astype(F32), i2.astype(F32), g1, g2, r1, r2)):
        meta = jnp.where(lane == c, val, meta)
    meta_ref[0] = meta.T[:ROUTER_FIELDS]


def _gather_kernel(ij_ref, is_ref, ifl_ref, h_ref, p0_ref, p1_ref, g0_ref, g1_ref, xs_ref, gs_ref, *, tg, ts):
    i = pl.program_id(0)
    fl = ifl_ref[i]

    @pl.when((fl & 2) != 0)
    def _():
        xs_ref[...] = jnp.zeros(xs_ref.shape, xs_ref.dtype)
        gs_ref[...] = jnp.zeros(gs_ref.shape, gs_ref.dtype)

    @pl.when((fl & 1) != 0)
    def _():
        rio = lax.broadcasted_iota(I32, (tg, ts), 0) + ij_ref[i] * tg
        m0 = p0_ref[0] == rio
        m1 = p1_ref[0] == rio
        sel = jnp.where(m0 | m1, 1.0, 0.0).astype(BF16)
        xs_ref[...] += jnp.dot(sel, h_ref[...], preferred_element_type=F32).astype(xs_ref.dtype)
        gs_ref[...] += jnp.sum(jnp.where(m0, g0_ref[0], 0.0) + jnp.where(m1, g1_ref[0], 0.0),
                               axis=-1, keepdims=True)


def _expert_kernel(be_ref, nv_ref, xs_ref, gs_ref, wg_ref, wu_ref, wd_ref, y_ref, acc_sc):
    j = pl.program_id(0)
    f = pl.program_id(1)
    valid = j < nv_ref[0]

    @pl.when(f == 0)
    def _():
        acc_sc[...] = jnp.zeros(acc_sc.shape, F32)

    @pl.when(valid)
    def _():
        x = xs_ref[...]
        hg = jnp.dot(x, wg_ref[...], preferred_element_type=F32)
        hu = jnp.dot(x, wu_ref[...], preferred_element_type=F32)
        mid = (jax.nn.silu(hg) * hu).astype(BF16)
        acc_sc[...] += jnp.dot(mid, wd_ref[...], preferred_element_type=F32)

    @pl.when(f == pl.num_programs(1) - 1)
    def _():
        y_ref[...] = jnp.where(valid, acc_sc[...] * gs_ref[...], 0.0).astype(y_ref.dtype)


def _combine_kernel(cj_ref, cs_ref, cfl_ref, x_ref, y_ref, p0_ref, p1_ref, o_ref, *, tg, ts):
    i = pl.program_id(0)
    fl = cfl_ref[i]

    @pl.when((fl & 2) != 0)
    def _():
        o_ref[...] = x_ref[...]

    @pl.when((fl & 1) != 0)
    def _():
        rio = lax.broadcasted_iota(I32, (tg, ts), 0) + cj_ref[i] * tg
        sel = jnp.where((p0_ref[0] == rio) | (p1_ref[0] == rio), 1.0, 0.0).astype(BF16)
        o_ref[...] += lax.dot_general(sel, y_ref[...], (((0,), (0,)), ((), ())), preferred_element_type=F32)


def _moe(x, hn, router_w, wg, wu, wd):
    t = x.shape[0]
    dffe = wg.shape[2]
    ts = min(512, t)
    tg = min(256, t)
    te = min(1024, t)
    tf = 512
    assert t % ts == 0 and te % tg == 0 and dffe % tf == 0
    nb = t // ts
    n_eblk = (2 * t) // te + N_EXPERTS
    ns = n_eblk * te
    n_gblk = ns // tg
    n_items = n_gblk + N_EXPERTS * nb

    w_pad = jnp.zeros((D_MODEL, LANES), BF16).at[:, :N_EXPERTS].set(router_w.astype(BF16))
    tri = (jnp.arange(ts)[:, None] > jnp.arange(ts)[None, :]).astype(BF16)
    meta, cum, tot = pl.pallas_call(
        _router_kernel,
        grid=(nb,),
        in_specs=[pl.BlockSpec((ts, D_MODEL), lambda i: (i, 0)), pl.BlockSpec((D_MODEL, LANES), lambda i: (0, 0)),
                  pl.BlockSpec((ts, ts), lambda i: (0, 0))],
        out_specs=[pl.BlockSpec((1, ROUTER_FIELDS, ts), lambda i: (i, 0, 0)),
                   pl.BlockSpec((1, 1, LANES), lambda i: (i, 0, 0)), pl.BlockSpec((1, LANES), lambda i: (0, 0))],
        out_shape=[jax.ShapeDtypeStruct((nb, ROUTER_FIELDS, ts), F32), jax.ShapeDtypeStruct((nb, 1, LANES), F32),
                   jax.ShapeDtypeStruct((1, LANES), F32)],
        scratch_shapes=[pltpu.VMEM((1, LANES), F32)],
        compiler_params=_cparams("arbitrary"),
        name="moe_router",
    )(hn, w_pad, tri)

    e1 = meta[:, 0, :].astype(I32)
    e2 = meta[:, 1, :].astype(I32)
    g1 = meta[:, 2, :]
    g2 = meta[:, 3, :]
    r1 = meta[:, 4, :].astype(I32)
    r2 = meta[:, 5, :].astype(I32)
    counts = tot[0, :N_EXPERTS].astype(I32)
    cum_e = cum[:, 0, :N_EXPERTS].astype(I32)
    gpad = ((counts + te - 1) // te) * te
    gend = jnp.cumsum(gpad)
    gstart = gend - gpad

    def group_start(e):
        out = jnp.zeros(e.shape, I32)
        for ex in range(N_EXPERTS):
            out = jnp.where(e == ex, gstart[ex], out)
        return out

    pos0 = group_start(e1) + r1
    pos1 = group_start(e2) + r2
    nvalid_e = (gend[-1] // te).astype(I32)
    blk_e = jnp.arange(n_eblk, dtype=I32) * te
    be = jnp.minimum(jnp.sum(gend[None, :] <= blk_e[:, None], axis=1), N_EXPERTS - 1).astype(I32)
    be = jnp.where(jnp.arange(n_eblk) < nvalid_e, be, be[jnp.maximum(nvalid_e - 1, 0)])
    gb0 = jnp.arange(n_gblk, dtype=I32) * tg
    gb_valid = gb0 < gend[-1]
    gb_e = be[gb0 // te]
    rank0 = gb0 - gstart[gb_e]
    cnt_e = counts[gb_e]
    rank_last = jnp.maximum(jnp.minimum(rank0 + tg, cnt_e) - 1, 0)
    cum_cols = cum_e[:, gb_e]
    lo = jnp.sum(cum_cols <= jnp.minimum(rank0, rank_last)[None, :], axis=0) - 1
    hi = jnp.sum(cum_cols <= rank_last[None, :], axis=0) - 1
    n_it = jnp.where(gb_valid, hi - lo + 1, 0)
    it_end = jnp.cumsum(n_it)
    it_start = it_end - n_it
    total = it_end[-1]
    ii = jnp.arange(n_items, dtype=I32)
    live = ii < total
    iic = jnp.minimum(ii, total - 1)
    it_j = jnp.sum(it_end[None, :] <= iic[:, None], axis=1).astype(I32)
    it_s = (lo[it_j] + iic - it_start[it_j]).astype(I32)
    it_first = (iic == it_start[it_j]) & live
    it_fl = (live.astype(I32) + 2 * it_first.astype(I32)).astype(I32)
    order = jnp.argsort(jnp.where(live, it_s, nb), stable=True)
    cj = it_j[order]
    cs = it_s[order]
    cl = live[order]
    cs = jnp.where(cl, cs, cs[jnp.maximum(total - 1, 0)])
    cj = jnp.where(cl, cj, cj[jnp.maximum(total - 1, 0)])
    c_first = cl & jnp.concatenate([jnp.ones((1,), bool), cs[1:] != cs[:-1]])
    c_fl = (cl.astype(I32) + 2 * c_first.astype(I32)).astype(I32)

    row3 = lambda v: v.reshape(nb, 1, ts)

    gspec = pltpu.PrefetchScalarGridSpec(
        num_scalar_prefetch=3,
        grid=(n_items,),
        in_specs=[pl.BlockSpec((ts, D_MODEL), lambda i, ij, is_, fl: (is_[i], 0))]
        + [pl.BlockSpec((1, 1, ts), lambda i, ij, is_, fl: (is_[i], 0, 0)) for _ in range(4)],
        out_specs=[pl.BlockSpec((tg, D_MODEL), lambda i, ij, is_, fl: (ij[i], 0)),
                   pl.BlockSpec((tg, 1), lambda i, ij, is_, fl: (ij[i], 0))],
    )
    xs, gs = pl.pallas_call(
        functools.partial(_gather_kernel, tg=tg, ts=ts),
        grid_spec=gspec,
        out_shape=[jax.ShapeDtypeStruct((ns, D_MODEL), BF16), jax.ShapeDtypeStruct((ns, 1), F32)],
        compiler_params=_cparams("arbitrary"),
        name="moe_gather",
    )(it_j, it_s, it_fl, hn, row3(pos0), row3(pos1), row3(g1), row3(g2))

    espec = pltpu.PrefetchScalarGridSpec(
        num_scalar_prefetch=2,
        grid=(n_eblk, dffe // tf),
        in_specs=[pl.BlockSpec((te, D_MODEL), lambda j, f, be_, nv: (j, 0)),
                  pl.BlockSpec((te, 1), lambda j, f, be_, nv: (j, 0)),
                  pl.BlockSpec((None, D_MODEL, tf), lambda j, f, be_, nv: (be_[j], 0, f)),
                  pl.BlockSpec((None, D_MODEL, tf), lambda j, f, be_, nv: (be_[j], 0, f)),
                  pl.BlockSpec((None, tf, D_MODEL), lambda j, f, be_, nv: (be_[j], f, 0))],
        out_specs=pl.BlockSpec((te, D_MODEL), lambda j, f, be_, nv: (j, 0)),
        scratch_shapes=[pltpu.VMEM((te, D_MODEL), F32)],
    )
    y = pl.pallas_call(
        _expert_kernel,
        grid_spec=espec,
        out_shape=jax.ShapeDtypeStruct((ns, D_MODEL), BF16),
        compiler_params=_cparams("parallel", "arbitrary"),
        name="moe_experts",
    )(be, nvalid_e.reshape(1), xs, gs, wg, wu, wd)

    cspec = pltpu.PrefetchScalarGridSpec(
        num_scalar_prefetch=3,
        grid=(n_items,),
        in_specs=[pl.BlockSpec((ts, D_MODEL), lambda i, cj_, cs_, fl: (cs_[i], 0)),
                  pl.BlockSpec((tg, D_MODEL), lambda i, cj_, cs_, fl: (cj_[i], 0)),
                  pl.BlockSpec((1, 1, ts), lambda i, cj_, cs_, fl: (cs_[i], 0, 0)),
                  pl.BlockSpec((1, 1, ts), lambda i, cj_, cs_, fl: (cs_[i], 0, 0))],
        out_specs=pl.BlockSpec((ts, D_MODEL), lambda i, cj_, cs_, fl: (cs_[i], 0)),
    )
    return pl.pallas_call(
        functools.partial(_combine_kernel, tg=tg, ts=ts),
        grid_spec=cspec,
        out_shape=jax.ShapeDtypeStruct((t, D_MODEL), F32),
        compiler_params=_cparams("arbitrary"),
        name="moe_combine",
    )(cj, cs, c_fl, x, y, row3(pos0), row3(pos1))


def _layer(x, b, l, li, p, mem_k, mem_v, past, kv_stack=None):
    h = _rms(x, p["norm1"])
    w_in = p["w_in"]
    seg = lambda a: w_in[:, SPLITS[a]:(SPLITS[a + 1] if a + 1 < len(SPLITS) else None)]
    tile = lambda g, n: jnp.tile(g.astype(F32), n)
    (q,) = _mm(h, seg(0), epi="seg64", gain=tile(p["q_norm_a"], 2 * H_A) * (DH_A ** -0.5 * LOG2E),
               layouts=("head",), bl=(b, l))
    k_stack = None if kv_stack is None else (li, kv_stack[0], kv_stack[1])
    v_stack = None if kv_stack is None else (li, kv_stack[0], kv_stack[2])
    tok = "tok" if kv_stack is None else "stack"
    k32, kbf = _mm(h, seg(1), epi="seg64", gain=tile(p["k_norm_a"], 2 * H_A), out_dtypes=(F32, BF16),
                   layouts=(tok, "head"), bl=(b, l), stack=k_stack)
    v32, vbf = _mm(h, seg(2), out_dtypes=(F32, BF16), layouts=(tok, "head"), bl=(b, l), stack=v_stack)
    (u_tm,) = _mm(h, seg(3), out_dtypes=(F32,), layouts=("time",), bl=(b, l))
    (qm,) = _mm(h, _pad_heads(seg(4)), epi="pad192", gain=_pad_heads(tile(p["q_norm_m"], H_M) * (DH_M ** -0.5)))
    (gates,) = _mm(h, seg(5), epi="sigmoid", tn=1024)

    lam_init = 0.8 - 0.6 * math.exp(-0.3 * li)
    lam = (jnp.exp(jnp.sum(p["lam_q1"].astype(F32) * p["lam_k1"].astype(F32)))
           - jnp.exp(jnp.sum(p["lam_q2"].astype(F32) * p["lam_k2"].astype(F32))) + lam_init)
    if past is None:
        o_a = _attn_prompt(q, kbf, vbf, p["rel_bias"], p["subln_a"], lam, lam_init, b, l)
        zero = jnp.zeros((b, N_GROUPS, STATE_P), F32)
        y_b, s_re, s_im = _s5(u_tm, p, zero, zero, b, l)
    else:
        cache_k, cache_v, s0_re, s0_im = past
        o_a = _attn_sample(q, kbf, vbf, cache_k, cache_v, li, p["rel_bias"], p["subln_a"], lam, lam_init, b, l)
        y_b, s_re, s_im = _s5(u_tm, p, s0_re, s0_im, b, l)
    o_m = _memattn(qm, mem_k, mem_v, b, l)
    w_om_pad = _pad_heads(p["w_om"].T).T
    x, hn = _merge(x, o_a, y_b, o_m, gates, p["w_oa"], p["w_ob"], w_om_pad, p["w_out"], p["norm2"], b, l)
    if li % 2 == 0:
        x = _ffn(x, hn, p["ffn_w_gate"], p["ffn_w_up"], p["ffn_w_down"])
    else:
        x = _moe(x, hn, p["router"], p["moe_w_gate"], p["moe_w_up"], p["moe_w_down"])
    return x, k32, v32, s_re, s_im


def kernel(x_prompt, x_sample, mem_prompt, cache_attn_k, cache_attn_v, state_ssm_re, state_ssm_im, cache_mem_k, cache_mem_v, norm1, norm2, w_in, q_norm_a, k_norm_a, lam_q1, lam_k1, lam_q2, lam_k2, subln_a, w_oa, lambda_re, lambda_im, log_dt, b_re, b_im, c_re, c_im, d_skip, w_glu, b_glu, w_ob, w_mk, w_mv, q_norm_m, k_norm_m, w_om, w_out, rel_bias, ffn_w_gate, ffn_w_up, ffn_w_down, router, moe_w_gate, moe_w_up, moe_w_down):
    depth = w_in.shape[0]
    bp, lp, _ = x_prompt.shape
    bs, ls, _ = x_sample.shape
    n_mem = mem_prompt.shape[1]
    xp = x_prompt.reshape(bp * lp, D_MODEL)
    xs = x_sample.reshape(bs * ls, D_MODEL)
    mem_bf = mem_prompt.reshape(bp * n_mem, D_MODEL).astype(BF16)
    outs = {name: [] for name in ("srp", "sip", "mkp", "mvp", "ks", "vs", "srs", "sis")}
    kp_buf = vp_buf = None
    for li in range(depth):
        j = li // 2
        p = {
            "norm1": norm1[li], "norm2": norm2[li], "w_in": w_in[li].astype(BF16),
            "q_norm_a": q_norm_a[li], "k_norm_a": k_norm_a[li],
            "lam_q1": lam_q1[li], "lam_k1": lam_k1[li], "lam_q2": lam_q2[li], "lam_k2": lam_k2[li],
            "subln_a": subln_a[li], "w_oa": w_oa[li].astype(BF16),
            "lambda_re": lambda_re[li], "lambda_im": lambda_im[li], "log_dt": log_dt[li],
            "b_re": b_re[li], "b_im": b_im[li], "c_re": c_re[li], "c_im": c_im[li],
            "d_skip": d_skip[li], "w_glu": w_glu[li], "b_glu": b_glu[li], "w_ob": w_ob[li].astype(BF16),
            "q_norm_m": q_norm_m[li], "w_om": w_om[li].astype(BF16), "w_out": w_out[li].astype(BF16),
            "rel_bias": rel_bias,
        }
        if li % 2 == 0:
            p.update(ffn_w_gate=ffn_w_gate[j].astype(BF16), ffn_w_up=ffn_w_up[j].astype(BF16),
                     ffn_w_down=ffn_w_down[j].astype(BF16))
        else:
            p.update(router=router[j], moe_w_gate=moe_w_gate[j].astype(BF16), moe_w_up=moe_w_up[j].astype(BF16),
                     moe_w_down=moe_w_down[j].astype(BF16))
        mk32, mkbf = _mm(mem_bf, w_mk[li].astype(BF16), epi="seg192", gain=jnp.tile(k_norm_m[li].astype(F32), H_M),
                         out_dtypes=(F32, BF16))
        mv32, mvbf = _mm(mem_bf, w_mv[li].astype(BF16), out_dtypes=(F32, BF16))
        xp, kp_buf, vp_buf, sr_p, si_p = _layer(xp, bp, lp, li, p, _pad_heads(mkbf.reshape(bp, n_mem, W_M)),
                                                _pad_heads(mvbf.reshape(bp, n_mem, W_M)), None,
                                                kv_stack=(depth, kp_buf, vp_buf))
        past = (cache_attn_k, cache_attn_v, state_ssm_re[li], state_ssm_im[li])
        xs, k_s, v_s, sr_s, si_s = _layer(xs, bs, ls, li, p,
                                          _pad_heads(cache_mem_k[li].reshape(bs, n_mem, W_M).astype(BF16)),
                                          _pad_heads(cache_mem_v[li].reshape(bs, n_mem, W_M).astype(BF16)), past)
        outs["srp"].append(sr_p)
        outs["sip"].append(si_p)
        outs["mkp"].append(mk32.reshape(bp, n_mem, H_M, DH_M))
        outs["mvp"].append(mv32.reshape(bp, n_mem, H_M, DH_M))
        outs["ks"].append(k_s.reshape(bs, ls, H_A, DV_A))
        outs["vs"].append(v_s.reshape(bs, ls, H_A, DV_A))
        outs["srs"].append(sr_s)
        outs["sis"].append(si_s)
    st = lambda name: jnp.stack(outs[name])
    kv_shape = (depth, bp, lp, H_A, DV_A)
    return (xp.reshape(bp, lp, D_MODEL), xs.reshape(bs, ls, D_MODEL),
            kp_buf.reshape(kv_shape), vp_buf.reshape(kv_shape), st("srp"), st("sip"),
            st("mkp"), st("mvp"), st("ks"), st("vs"), st("srs"), st("sis"))
```

```python
import functools
import math

import jax
import jax.numpy as jnp
from jax import lax
from jax.experimental import pallas as pl
from jax.experimental.pallas import tpu as pltpu

F32 = jnp.float32
BF16 = jnp.bfloat16
I32 = jnp.int32

D_MODEL = 1024
CHUNK = 64
H_A = 8
DH_A = 64
DV_A = 2 * DH_A
W_A = H_A * DV_A
GROUP_CH = 16
N_GROUPS = 48
STATE_P = 64
W_B = N_GROUPS * GROUP_CH
H_M = 4
DH_M = 192
W_M = H_M * DH_M
DH_M_PAD = 256
W_M_PAD = H_M * DH_M_PAD
REL_BUCKETS = 32
REL_MAX_DIST = 128
N_EXPERTS = 8
ROUTER_FIELDS = 8
EPS = 1e-6
NEG_INF = -1e30
LOG2E = math.log2(math.e)
Q_A_W = H_A * 2 * DH_A
SPLITS = (0, Q_A_W, 2 * Q_A_W, 2 * Q_A_W + W_A, 2 * Q_A_W + W_A + W_B, 2 * Q_A_W + W_A + W_B + W_M)
GATE_W = 3 * D_MODEL

LANES = 128
VMEM_LIMIT = 56 * 1024 * 1024
ATTN_GROUP = 4
S5_SLAB_GROUPS = LANES // GROUP_CH
S5_SLABS = N_GROUPS // S5_SLAB_GROUPS
S5_SLAB_STATE = S5_SLAB_GROUPS * STATE_P


def _cparams(*sem):
    return pltpu.CompilerParams(dimension_semantics=sem, vmem_limit_bytes=VMEM_LIMIT)


def _rms_kernel(x_ref, g_ref, o_ref):
    x = x_ref[...]
    ms = jnp.mean(x * x, axis=-1, keepdims=True)
    o_ref[...] = ((x * lax.rsqrt(ms + EPS)) * g_ref[...]).astype(o_ref.dtype)


def _rms(x, g):
    t, d = x.shape
    tm = min(1024, t)
    return pl.pallas_call(
        _rms_kernel,
        grid=(t // tm,),
        in_specs=[pl.BlockSpec((tm, d), lambda i: (i, 0)), pl.BlockSpec((1, d), lambda i: (0, 0))],
        out_specs=pl.BlockSpec((tm, d), lambda i: (i, 0)),
        out_shape=jax.ShapeDtypeStruct((t, d), BF16),
        compiler_params=_cparams("parallel"),
        name="rmsnorm",
    )(x, g.reshape(1, d).astype(F32))


def _seg64_scale(ys):
    lane = lax.broadcasted_iota(I32, (1, LANES), 1)
    left = lane < DH_A
    y2 = ys * ys
    sl = jnp.sum(jnp.where(left, y2, 0.0), axis=-1, keepdims=True)
    sr = jnp.sum(jnp.where(left, 0.0, y2), axis=-1, keepdims=True)
    rl = lax.rsqrt(sl * (1.0 / DH_A) + EPS)
    rr = lax.rsqrt(sr * (1.0 / DH_A) + EPS)
    return jnp.where(left, rl, rr)


def _seg192_scale(y):
    col = lax.broadcasted_iota(I32, (1, W_M), 1)
    y2 = y * y
    rb = jnp.zeros_like(y)
    for h in range(H_M):
        m = (col >= DH_M * h) & (col < DH_M * (h + 1))
        s = jnp.sum(jnp.where(m, y2, 0.0), axis=-1, keepdims=True)
        rb = jnp.where(m, lax.rsqrt(s / DH_M + EPS), rb)
    return rb


def _mm_kernel(*refs, epi, layouts, has_gain, fill_li=None):
    a_ref, w_ref = refs[0], refs[1]
    g_ref = refs[2] if has_gain else None
    outs = list(refs[len(refs) - len(layouts):])
    for k, lay in enumerate(layouts):
        if lay == "stack_fill":
            o = outs[k]
            for d in range(o.shape[0]):
                if d != fill_li:
                    o[d] = jnp.zeros(o.shape[1:], o.dtype)
            outs[k] = o.at[fill_li]
    y = jnp.dot(a_ref[...], w_ref[...], preferred_element_type=F32)
    tn = y.shape[1]
    if epi == "seg192":
        y = (y * _seg192_scale(y)) * g_ref[...]
    elif epi == "pad192":
        for h in range(tn // DH_M_PAD):
            sl = slice(h * DH_M_PAD, (h + 1) * DH_M_PAD)
            ys = y[:, sl]
            r = lax.rsqrt(jnp.sum(ys * ys, axis=-1, keepdims=True) / DH_M + EPS)
            for o in outs:
                o[:, sl] = ((ys * r) * g_ref[:, sl]).astype(o.dtype)
        return
    elif epi == "sigmoid":
        y = jax.nn.sigmoid(y)
    if epi == "seg64" or "head" in layouts:
        for s in range(tn // LANES):
            sl = slice(s * LANES, (s + 1) * LANES)
            ys = y[:, sl]
            if epi == "seg64":
                ys = (ys * _seg64_scale(ys)) * g_ref[:, sl]
            for o, lay in zip(outs, layouts):
                if lay == "head":
                    o[s] = ys.astype(o.dtype)
                else:
                    o[:, sl] = ys.astype(o.dtype)
        return
    for o in outs:
        o[...] = y.astype(o.dtype)


def _mm(a, w, *, epi="plain", gain=None, out_dtypes=(BF16,), layouts=None, tn=None, bl=None, stack=None):
    t, k = a.shape
    n = w.shape[1]
    tn = n if tn is None else tn
    layouts = ("tok",) * len(out_dtypes) if layouts is None else layouts
    assert n % tn == 0 and tn % LANES == 0
    if bl is None:
        assert all(lay in ("tok", "stack") for lay in layouts)
        tm, nl = min(1024, t), 1
    else:
        b, l = bl
        tm = min(1024, l)
        assert l % tm == 0
        nl = l // tm
    assert t % tm == 0
    out_specs, out_shapes = [], []
    alias_args, aliases = [], {}
    n_in = 2 + int(gain is not None)
    for oi, (dt, lay) in enumerate(zip(out_dtypes, layouts)):
        if lay == "tok":
            out_specs.append(pl.BlockSpec((tm, tn), lambda i, j: (i, j)))
            out_shapes.append(jax.ShapeDtypeStruct((t, n), dt))
        elif lay == "stack":
            li, depth, buf = stack
            out_shapes.append(jax.ShapeDtypeStruct((depth, t, n), dt))
            if buf is None:
                out_specs.append(pl.BlockSpec((depth, tm, tn), lambda i, j: (0, i, j)))
                layouts = tuple("stack_fill" if k == oi else lay_k for k, lay_k in enumerate(layouts))
            else:
                out_specs.append(pl.BlockSpec((None, tm, tn), lambda i, j: (li, i, j)))
                aliases[n_in + len(alias_args)] = oi
                alias_args.append(buf)
        elif lay == "time":
            assert tn == n
            out_specs.append(pl.BlockSpec((tm, tn), lambda i, j: (i % nl, i // nl)))
            out_shapes.append(jax.ShapeDtypeStruct((l, b * n), dt))
        else:
            out_specs.append(pl.BlockSpec((None, tn // LANES, tm, LANES), lambda i, j: (i // nl, j, i % nl, 0)))
            out_shapes.append(jax.ShapeDtypeStruct((b, n // LANES, l, LANES), dt))
    in_specs = [pl.BlockSpec((tm, k), lambda i, j: (i, 0)), pl.BlockSpec((k, tn), lambda i, j: (0, j))]
    args = [a, w]
    if gain is not None:
        in_specs.append(pl.BlockSpec((1, tn), lambda i, j: (0, j)))
        args.append(gain.reshape(1, n).astype(F32))
    in_specs += [pl.BlockSpec(memory_space=pl.ANY) for _ in alias_args]
    return pl.pallas_call(
        functools.partial(_mm_kernel, epi=epi, layouts=tuple(layouts), has_gain=gain is not None,
                          fill_li=None if stack is None else stack[0]),
        grid=(t // tm, n // tn),
        in_specs=in_specs,
        out_specs=out_specs,
        out_shape=out_shapes,
        input_output_aliases=aliases,
        compiler_params=_cparams("parallel", "parallel"),
        name="mm_" + epi,
    )(*args, *alias_args)


def _rel_bias(q_pos, k_pos, table):
    rel = k_pos[None, :] - q_pos[:, None]
    half = REL_BUCKETS // 2
    max_exact = half // 2
    n = jnp.abs(rel)
    nf = jnp.maximum(n, 1).astype(F32)
    large = max_exact + (jnp.log(nf / max_exact) / math.log(REL_MAX_DIST / max_exact)
                         * (half - max_exact)).astype(I32)
    large = jnp.minimum(large, half - 1)
    bucket = jnp.where(rel > 0, half, 0) + jnp.where(n < max_exact, n, large)
    tab = table.astype(F32)
    bias = jnp.zeros((H_A,) + bucket.shape, F32)
    for bk in range(REL_BUCKETS):
        bias = jnp.where((bucket == bk)[None], tab[bk][:, None, None], bias)
    visible = (k_pos[None, :] // CHUNK) <= (q_pos[:, None] // CHUNK)
    return jnp.where(visible[None], bias, NEG_INF)


def _split_halves(q):
    lane = lax.broadcasted_iota(I32, (1, LANES), 1)
    zero = jnp.zeros_like(q)
    return jnp.concatenate([jnp.where(lane < DH_A, q, zero), jnp.where(lane >= DH_A, q, zero)], axis=0)


def _subln(o, g, lam_init):
    ms = jnp.mean(o * o, axis=-1, keepdims=True)
    return ((o * lax.rsqrt(ms + EPS)) * g) * (1.0 - lam_init)


def _attn_prompt_kernel(lam_ref, q_ref, k_ref, v_ref, bias_ref, g_ref, o_ref,
                        s_sc, m_sc, acc_sc, *, tq, lam_init):
    qi = pl.program_id(2)
    n_blk = qi + 1
    qs = _split_halves(q_ref[...])
    nt = (((1,), (1,)), ((), ()))
    nc = tq // LANES

    def fold(op, s):
        r = s[:, :LANES]
        for c in range(1, nc):
            r = op(r, s[:, c * LANES:(c + 1) * LANES])
        return r

    def score(j):
        r0 = pl.multiple_of(j * tq, tq)
        bias = bias_ref[jnp.clip(j - (qi - 2), 0, 2)]
        s = lax.dot_general(qs, k_ref[pl.ds(r0, tq), :], nt, preferred_element_type=F32)
        s = s + jnp.concatenate([bias, bias], axis=0)
        s_sc[j] = s
        return fold(jnp.maximum, s)

    def sweep(group_fn):
        n_full = n_blk // ATTN_GROUP

        def trip(g, c):
            group_fn(ATTN_GROUP * g, ATTN_GROUP)
            return c

        lax.fori_loop(0, n_full, trip, 0)
        rem = n_blk - ATTN_GROUP * n_full

        @pl.when(rem >= 2)
        def _():
            group_fn(ATTN_GROUP * n_full, 2)

        @pl.when(rem % 2 == 1)
        def _():
            group_fn(qi, 1)

    m_sc[...] = jnp.full(m_sc.shape, -jnp.inf, F32)

    def score_group(j0, g):
        m_sc[...] = jnp.maximum(m_sc[...], functools.reduce(jnp.maximum, [score(j0 + t) for t in range(g)]))

    sweep(score_group)
    m = jnp.max(m_sc[...], axis=-1, keepdims=True)
    m_sc[...] = jnp.broadcast_to(m, m_sc.shape)
    acc_sc[...] = jnp.zeros(acc_sc.shape, F32)
    ones = jnp.ones((tq, DV_A), BF16)

    def pv(j):
        mrow = m_sc[...]
        s = s_sc[j]
        p = jnp.concatenate([jnp.exp2(s[:, cc * LANES:(cc + 1) * LANES] - mrow) for cc in range(nc)], axis=1)
        r0 = pl.multiple_of(j * tq, tq)
        v_ext = jnp.concatenate([v_ref[pl.ds(r0, tq), :], ones], axis=1)
        return jnp.dot(p.astype(BF16), v_ext, preferred_element_type=F32)

    def pv_group(j0, g):
        acc_sc[...] += functools.reduce(jnp.add, [pv(j0 + t) for t in range(g)])

    sweep(pv_group)
    acc = acc_sc[...]
    o = acc[:, :DV_A] / acc[:, DV_A:]
    o = o[:tq] - lam_ref[0] * o[tq:]
    o_ref[...] = _subln(o, g_ref[...], lam_init).astype(o_ref.dtype)


def _attn_prompt(q, k, v, rel_table, subln_g, lam, lam_init, b, l):
    tq = min(512, l)
    assert l % tq == 0 and tq % CHUNK == 0 and tq >= 128
    cfar = _rel_bias(jnp.array([2 * tq + 1]), jnp.array([0]), rel_table)[:, 0, 0]
    bias = _rel_bias(jnp.arange(tq, 2 * tq), jnp.arange(2 * tq), rel_table)
    bias = (bias - cfar[:, None, None]) * LOG2E
    bias = jnp.stack([jnp.zeros_like(bias[:, :, :tq]), bias[:, :, :tq], bias[:, :, tq:]], axis=1)
    smem = pl.BlockSpec(memory_space=pltpu.SMEM)
    out = pl.pallas_call(
        functools.partial(_attn_prompt_kernel, tq=tq, lam_init=lam_init),
        grid=(b, H_A, l // tq),
        in_specs=[smem,
                  pl.BlockSpec((None, None, tq, DV_A), lambda bi, h, i: (bi, h, i, 0)),
                  pl.BlockSpec((None, None, l, DV_A), lambda bi, h, i: (bi, h, 0, 0)),
                  pl.BlockSpec((None, None, l, DV_A), lambda bi, h, i: (bi, h, 0, 0)),
                  pl.BlockSpec((None, 3, tq, tq), lambda bi, h, i: (h, 0, 0, 0)),
                  pl.BlockSpec((1, DV_A), lambda bi, h, i: (0, 0))],
        out_specs=pl.BlockSpec((None, tq, DV_A), lambda bi, h, i: (bi, i, h)),
        out_shape=jax.ShapeDtypeStruct((b, l, W_A), BF16),
        scratch_shapes=[pltpu.VMEM((l // tq, 2 * tq, tq), F32), pltpu.VMEM((2 * tq, LANES), F32),
                        pltpu.VMEM((2 * tq, 2 * DV_A), F32)],
        compiler_params=_cparams("parallel", "parallel", "arbitrary"),
        name="attn_prompt",
    )(lam.reshape(1), q, k, v, bias, subln_g.reshape(1, DV_A).astype(F32))
    return out.reshape(b * l, W_A)


def _attn_sample_kernel(lam_ref, q_ref, kp_ref, vp_ref, kn_ref, vn_ref, bp_ref, bn_ref, g_ref, o_ref,
                        *, lq, lam_init):
    lam = lam_ref[0]
    nt = (((1,), (1,)), ((), ()))
    for h in range(H_A):
        sl = slice(h * DV_A, (h + 1) * DV_A)
        qs = _split_halves(q_ref[h])
        kp = kp_ref[:, h, :].astype(BF16)
        vp = vp_ref[:, h, :].astype(BF16)
        bp = bp_ref[h]
        bn = bn_ref[h]
        sp = lax.dot_general(qs, kp, nt, preferred_element_type=F32) + jnp.concatenate([bp, bp], axis=0)
        sn = lax.dot_general(qs, kn_ref[h], nt, preferred_element_type=F32) + jnp.concatenate([bn, bn], axis=0)
        m = jnp.maximum(jnp.max(sp, axis=-1, keepdims=True), jnp.max(sn, axis=-1, keepdims=True))
        pp = jnp.exp2(sp - m)
        pn = jnp.exp2(sn - m)
        lsum = jnp.sum(pp, axis=-1, keepdims=True) + jnp.sum(pn, axis=-1, keepdims=True)
        o = (jnp.dot(pp.astype(BF16), vp, preferred_element_type=F32)
             + jnp.dot(pn.astype(BF16), vn_ref[h], preferred_element_type=F32)) / lsum
        o = o[:lq] - lam * o[lq:]
        o_ref[:, sl] = _subln(o, g_ref[...], lam_init).astype(o_ref.dtype)


def _attn_sample(q, k_new, v_new, cache_k, cache_v, li, rel_table, subln_g, lam, lam_init, b, l):
    past = cache_k.shape[2]
    k_pos = jnp.arange(past + l)
    bias = _rel_bias(k_pos[past:], k_pos, rel_table) * LOG2E
    bp, bn = bias[:, :, :past], bias[:, :, past:]
    smem = pl.BlockSpec(memory_space=pltpu.SMEM)
    tok = lambda rows: pl.BlockSpec((None, rows, W_A), lambda bi: (bi, 0, 0))
    head = pl.BlockSpec((None, H_A, l, DV_A), lambda bi: (bi, 0, 0, 0))
    cache = pl.BlockSpec((None, None, past, H_A, DV_A), lambda bi: (li, bi, 0, 0, 0))
    full = lambda shape: pl.BlockSpec(shape, lambda bi: tuple(0 for _ in shape))
    out = pl.pallas_call(
        functools.partial(_attn_sample_kernel, lq=l, lam_init=lam_init),
        grid=(b,),
        in_specs=[smem, head, cache, cache, head, head,
                  full((H_A, l, past)), full((H_A, l, l)), full((1, DV_A))],
        out_specs=tok(l),
        out_shape=jax.ShapeDtypeStruct((b, l, W_A), BF16),
        compiler_params=_cparams("parallel"),
        name="attn_sample",
    )(lam.reshape(1), q, cache_k, cache_v, k_new, v_new, bp, bn, subln_g.reshape(1, DV_A).astype(F32))
    return out.reshape(b * l, W_A)


def _pad_heads(z):
    lead = z.shape[:-1]
    z = z.reshape(*lead, H_M, DH_M)
    z = jnp.pad(z, [(0, 0)] * len(lead) + [(0, 0), (0, DH_M_PAD - DH_M)])
    return z.reshape(*lead, W_M_PAD)


def _memattn_kernel(q_ref, mk_ref, mv_ref, o_ref):
    nt = (((1,), (1,)), ((), ()))
    for h in range(H_M):
        sl = slice(h * DH_M_PAD, (h + 1) * DH_M_PAD)
        s = lax.dot_general(q_ref[:, sl], mk_ref[:, sl], nt, preferred_element_type=F32)
        m = jnp.max(s, axis=-1, keepdims=True)
        p = jnp.exp(s - m)
        p = p / jnp.sum(p, axis=-1, keepdims=True)
        o_ref[:, sl] = jnp.dot(p.astype(BF16), mv_ref[:, sl], preferred_element_type=F32).astype(o_ref.dtype)


def _memattn(qm, mk, mv, b, l):
    n_mem = mk.shape[1]
    tq = min(512, l)
    assert l % tq == 0
    out = pl.pallas_call(
        _memattn_kernel,
        grid=(b, l // tq),
        in_specs=[pl.BlockSpec((None, tq, W_M_PAD), lambda bi, i: (bi, i, 0)),
                  pl.BlockSpec((None, n_mem, W_M_PAD), lambda bi, i: (bi, 0, 0)),
                  pl.BlockSpec((None, n_mem, W_M_PAD), lambda bi, i: (bi, 0, 0))],
        out_specs=pl.BlockSpec((None, tq, W_M_PAD), lambda bi, i: (bi, i, 0)),
        out_shape=jax.ShapeDtypeStruct((b, l, W_M_PAD), BF16),
        compiler_params=_cparams("parallel", "parallel"),
        name="memattn",
    )(qm.reshape(b, l, W_M_PAD), mk, mv)
    return out.reshape(b * l, W_M_PAD)


def _s5_tables(p):
    dt = jnp.exp(p["log_dt"].astype(F32))[:, None]
    lr = jnp.minimum(p["lambda_re"].astype(F32), -1e-4)
    lim = p["lambda_im"].astype(F32)
    mag = jnp.exp(lr * dt)
    ar = mag * jnp.cos(lim * dt)
    ai = mag * jnp.sin(lim * dt)
    den = lr * lr + lim * lim
    fr = ((ar - 1.0) * lr + ai * lim) / den
    fi = (ai * lr - (ar - 1.0) * lim) / den
    br = p["b_re"].astype(F32)
    bi = p["b_im"].astype(F32)
    bbr = fr[..., None] * br - fi[..., None] * bi
    bbi = fr[..., None] * bi + fi[..., None] * br
    eye = jnp.eye(S5_SLAB_GROUPS, dtype=F32)
    sg = (S5_SLABS, S5_SLAB_GROUPS)

    def in_w(bb):
        w = jnp.einsum("sgpc,gh->sgchp", bb.reshape(*sg, STATE_P, GROUP_CH), eye)
        return w.reshape(S5_SLABS, LANES, S5_SLAB_STATE)

    def out_w(c):
        w = jnp.einsum("sgcp,gh->sgphc", c.reshape(*sg, GROUP_CH, STATE_P), eye)
        return w.reshape(S5_SLABS, S5_SLAB_STATE, LANES)

    wb = jnp.concatenate([in_w(bbr), in_w(bbi)], axis=2).astype(BF16)
    wc = jnp.concatenate([out_w(p["c_re"].astype(F32)), -out_w(p["c_im"].astype(F32))], axis=1).astype(BF16)
    a_re = ar.reshape(S5_SLABS, 1, S5_SLAB_STATE)
    a_im = ai.reshape(S5_SLABS, 1, S5_SLAB_STATE)
    return wb, wc, a_re, a_im


def _s5_kernel(u_ref, wb_ref, wc_ref, ar_ref, ai_ref, d_ref, wglu_ref, bglu_ref, s0r_ref, s0i_ref,
               y_ref, sr_ref, si_ref, bu_sc, x_sc, y_sc, *, tl, nb):
    i = pl.program_id(0)

    @pl.when(i == 0)
    def _():
        sr_ref[...] = s0r_ref[...]
        si_ref[...] = s0i_ref[...]

    rows = tl * nb
    u = u_ref[...].reshape(rows, W_B)
    y_sc[...] = d_ref[...] * u
    for s in range(S5_SLABS):
        sl = slice(s * LANES, (s + 1) * LANES)
        bu_sc[...] = jnp.dot(u[:, sl].astype(BF16), wb_ref[s], preferred_element_type=F32)
        ar = jnp.broadcast_to(ar_ref[s], (nb, S5_SLAB_STATE))
        ai = jnp.broadcast_to(ai_ref[s], (nb, S5_SLAB_STATE))

        def body(t, carry):
            sr, si = carry
            r0 = pl.multiple_of(t * nb, nb)
            b_r = bu_sc[pl.ds(r0, nb), :S5_SLAB_STATE]
            b_i = bu_sc[pl.ds(r0, nb), S5_SLAB_STATE:]
            nr = ar * sr - ai * si + b_r
            ni = ar * si + ai * sr + b_i
            x_sc[pl.ds(r0, nb), :S5_SLAB_STATE] = nr.astype(BF16)
            x_sc[pl.ds(r0, nb), S5_SLAB_STATE:] = ni.astype(BF16)
            return nr, ni

        sr, si = lax.fori_loop(0, tl, body, (sr_ref[s], si_ref[s]), unroll=True)
        sr_ref[s] = sr
        si_ref[s] = si
        y_sc[:, sl] = y_sc[:, sl] + jnp.dot(x_sc[...], wc_ref[s], preferred_element_type=F32)
    yg = jax.nn.gelu(y_sc[...])
    z = jnp.dot(yg.astype(BF16), wglu_ref[...], preferred_element_type=F32) + bglu_ref[...]
    out = yg * jax.nn.sigmoid(z)
    y_ref[...] = out.astype(y_ref.dtype).reshape(tl, nb, W_B)


def _s5(u_tm, p, s0_re, s0_im, b, l):
    assert b % 16 == 0
    tl = min(64, l)
    assert l % tl == 0
    wb, wc, a_re, a_im = _s5_tables(p)
    to_slab = lambda s0: jnp.transpose(s0.astype(F32).reshape(b, S5_SLABS, S5_SLAB_STATE), (1, 0, 2))
    from_slab = lambda st: jnp.transpose(st, (1, 0, 2)).reshape(b, N_GROUPS, STATE_P)
    full = lambda shape: pl.BlockSpec(shape, lambda i: tuple(0 for _ in shape))
    st_shape = (S5_SLABS, b, S5_SLAB_STATE)
    rows = tl * b
    y, sr, si = pl.pallas_call(
        functools.partial(_s5_kernel, tl=tl, nb=b),
        grid=(l // tl,),
        in_specs=[pl.BlockSpec((tl, b, W_B), lambda i: (i, 0, 0)),
                  full(wb.shape), full(wc.shape), full(a_re.shape), full(a_im.shape),
                  full((1, W_B)), full((W_B, W_B)), full((1, W_B)), full(st_shape), full(st_shape)],
        out_specs=[pl.BlockSpec((tl, b, W_B), lambda i: (i, 0, 0)), full(st_shape), full(st_shape)],
        out_shape=[jax.ShapeDtypeStruct((l, b, W_B), BF16),
                   jax.ShapeDtypeStruct(st_shape, F32), jax.ShapeDtypeStruct(st_shape, F32)],
        scratch_shapes=[pltpu.VMEM((rows, 2 * S5_SLAB_STATE), F32),
                        pltpu.VMEM((rows, 2 * S5_SLAB_STATE), BF16),
                        pltpu.VMEM((rows, W_B), F32)],
        compiler_params=_cparams("arbitrary"),
        name="s5_scan",
    )(u_tm.reshape(l, b, W_B), wb, wc, a_re, a_im,
      p["d_skip"].astype(F32).reshape(1, W_B), p["w_glu"].astype(BF16), p["b_glu"].astype(F32).reshape(1, W_B),
      to_slab(s0_re), to_slab(s0_im))
    return y.reshape(l, b * W_B), from_slab(sr), from_slab(si)


def _merge_kernel(x_ref, oa_ref, yb_ref, om_ref, gt_ref, woa_ref, wob_ref, wom_ref, wout_ref, g2_ref,
                  xo_ref, hn_ref):
    pa = jnp.dot(oa_ref[...], woa_ref[...], preferred_element_type=F32)
    pb = jnp.dot(yb_ref[...], wob_ref[...], preferred_element_type=F32)
    pm = jnp.dot(om_ref[...], wom_ref[...], preferred_element_type=F32)
    d = D_MODEL
    mix = (gt_ref[:, :d].astype(F32) * pa + gt_ref[:, d:2 * d].astype(F32) * pb
           + gt_ref[:, 2 * d:].astype(F32) * pm)
    x = x_ref[...] + jnp.dot(mix.astype(BF16), wout_ref[...], preferred_element_type=F32)
    xo_ref[...] = x
    ms = jnp.mean(x * x, axis=-1, keepdims=True)
    hn_ref[...] = ((x * lax.rsqrt(ms + EPS)) * g2_ref[...]).astype(hn_ref.dtype)


def _merge(x, o_a, y_b_tm, o_m, gates, w_oa, w_ob, w_om, w_out, g2, b, l):
    t = b * l
    tm = min(512, l)
    assert l % tm == 0
    nl = l // tm
    row = lambda w: pl.BlockSpec((tm, w), lambda bi, i: (bi * nl + i, 0))
    full = lambda shape: pl.BlockSpec(shape, lambda bi, i: (0, 0))
    return pl.pallas_call(
        _merge_kernel,
        grid=(b, nl),
        in_specs=[row(D_MODEL), row(W_A), pl.BlockSpec((tm, W_B), lambda bi, i: (i, bi)), row(W_M_PAD), row(GATE_W),
                  full(w_oa.shape), full(w_ob.shape), full(w_om.shape), full(w_out.shape), full((1, D_MODEL))],
        out_specs=[row(D_MODEL), row(D_MODEL)],
        out_shape=[jax.ShapeDtypeStruct((t, D_MODEL), F32), jax.ShapeDtypeStruct((t, D_MODEL), BF16)],
        compiler_params=_cparams("parallel", "parallel"),
        name="merge",
    )(x, o_a, y_b_tm, o_m, gates, w_oa, w_ob, w_om, w_out, g2.reshape(1, D_MODEL).astype(F32))


def _ffn_kernel(x_ref, h_ref, wg_ref, wu_ref, wd_ref, o_ref, acc_sc):
    f = pl.program_id(1)

    @pl.when(f == 0)
    def _():
        acc_sc[...] = x_ref[...]

    h = h_ref[...]
    hg = jnp.dot(h, wg_ref[...], preferred_element_type=F32)
    hu = jnp.dot(h, wu_ref[...], preferred_element_type=F32)
    mid = (jax.nn.silu(hg) * hu).astype(BF16)
    acc_sc[...] += jnp.dot(mid, wd_ref[...], preferred_element_type=F32)

    @pl.when(f == pl.num_programs(1) - 1)
    def _():
        o_ref[...] = acc_sc[...]


def _ffn(x, hn, wg, wu, wd):
    t = x.shape[0]
    dff = wg.shape[1]
    tm = min(512, t)
    tf = dff // 2 if (dff // 2) % LANES == 0 else dff
    return pl.pallas_call(
        _ffn_kernel,
        grid=(t // tm, dff // tf),
        in_specs=[pl.BlockSpec((tm, D_MODEL), lambda i, f: (i, 0)), pl.BlockSpec((tm, D_MODEL), lambda i, f: (i, 0)),
                  pl.BlockSpec((D_MODEL, tf), lambda i, f: (0, f)), pl.BlockSpec((D_MODEL, tf), lambda i, f: (0, f)),
                  pl.BlockSpec((tf, D_MODEL), lambda i, f: (f, 0))],
        out_specs=pl.BlockSpec((tm, D_MODEL), lambda i, f: (i, 0)),
        out_shape=jax.ShapeDtypeStruct((t, D_MODEL), F32),
        scratch_shapes=[pltpu.VMEM((tm, D_MODEL), F32)],
        compiler_params=_cparams("parallel", "arbitrary"),
        name="ffn_dense",
    )(x, hn, wg, wu, wd)


def _router_kernel(h_ref, w_ref, tri_ref, meta_ref, cum_ref, tot_ref, carry_sc):
    i = pl.program_id(0)

    @pl.when(i == 0)
    def _():
        carry_sc[...] = jnp.zeros(carry_sc.shape, F32)

    lane = lax.broadcasted_iota(I32, (1, LANES), 1)
    logits = jnp.dot(h_ref[...], w_ref[...], preferred_element_type=F32)
    lg = jnp.where(lane < N_EXPERTS, logits, -jnp.inf)
    m1 = jnp.max(lg, axis=-1, keepdims=True)
    i1 = jnp.min(jnp.where(lg == m1, lane, LANES), axis=-1, keepdims=True)
    lg2 = jnp.where(lane == i1, -jnp.inf, lg)
    m2 = jnp.max(lg2, axis=-1, keepdims=True)
    i2 = jnp.min(jnp.where(lg2 == m2, lane, LANES), axis=-1, keepdims=True)
    e = jnp.exp(m2 - m1)
    g1 = 1.0 / (1.0 + e)
    g2 = e / (1.0 + e)
    hit1 = lane == i1
    hit2 = lane == i2
    cnt = jnp.where(hit1 | hit2, 1.0, 0.0)
    carry = carry_sc[...]
    before = jnp.dot(tri_ref[...], cnt.astype(BF16), preferred_element_type=F32) + carry
    r1 = jnp.sum(jnp.where(hit1, before, 0.0), axis=-1, keepdims=True)
    r2 = jnp.sum(jnp.where(hit2, before, 0.0), axis=-1, keepdims=True)
    cum_ref[0] = carry
    carry = carry + jnp.sum(cnt, axis=0, keepdims=True)
    carry_sc[...] = carry
    tot_ref[...] = carry
    meta = jnp.zeros(logits.shape, F32)
    for c, val in enumerate((i1.astype(F32), i2.astype(F32), g1, g2, r1, r2)):
        meta = jnp.where(lane == c, val, meta)
    meta_ref[0] = meta.T[:ROUTER_FIELDS]


def _gather_kernel(ij_ref, is_ref, ifl_ref, h_ref, p0_ref, p1_ref, g0_ref, g1_ref, xs_ref, gs_ref, *, tg, ts):
    i = pl.program_id(0)
    fl = ifl_ref[i]

    @pl.when((fl & 2) != 0)
    def _():
        xs_ref[...] = jnp.zeros(xs_ref.shape, xs_ref.dtype)
        gs_ref[...] = jnp.zeros(gs_ref.shape, gs_ref.dtype)

    @pl.when((fl & 1) != 0)
    def _():
        rio = lax.broadcasted_iota(I32, (tg, ts), 0) + ij_ref[i] * tg
        m0 = p0_ref[0] == rio
        m1 = p1_ref[0] == rio
        sel = jnp.where(m0 | m1, 1.0, 0.0).astype(BF16)
        xs_ref[...] += jnp.dot(sel, h_ref[...], preferred_element_type=F32).astype(xs_ref.dtype)
        gs_ref[...] += jnp.sum(jnp.where(m0, g0_ref[0], 0.0) + jnp.where(m1, g1_ref[0], 0.0),
                               axis=-1, keepdims=True)


def _expert_kernel(be_ref, nv_ref, xs_ref, gs_ref, wg_ref, wu_ref, wd_ref, y_ref, acc_sc):
    j = pl.program_id(0)
    f = pl.program_id(1)
    valid = j < nv_ref[0]

    @pl.when(f == 0)
    def _():
        acc_sc[...] = jnp.zeros(acc_sc.shape, F32)

    @pl.when(valid)
    def _():
        x = xs_ref[...]
        hg = jnp.dot(x, wg_ref[...], preferred_element_type=F32)
        hu = jnp.dot(x, wu_ref[...], preferred_element_type=F32)
        mid = (jax.nn.silu(hg) * hu).astype(BF16)
        acc_sc[...] += jnp.dot(mid, wd_ref[...], preferred_element_type=F32)

    @pl.when(f == pl.num_programs(1) - 1)
    def _():
        y_ref[...] = jnp.where(valid, acc_sc[...] * gs_ref[...], 0.0).astype(y_ref.dtype)


def _combine_kernel(cj_ref, cs_ref, cfl_ref, x_ref, y_ref, p0_ref, p1_ref, o_ref, *, tg, ts):
    i = pl.program_id(0)
    fl = cfl_ref[i]

    @pl.when((fl & 2) != 0)
    def _():
        o_ref[...] = x_ref[...]

    @pl.when((fl & 1) != 0)
    def _():
        rio = lax.broadcasted_iota(I32, (tg, ts), 0) + cj_ref[i] * tg
        sel = jnp.where((p0_ref[0] == rio) | (p1_ref[0] == rio), 1.0, 0.0).astype(BF16)
        o_ref[...] += lax.dot_general(sel, y_ref[...], (((0,), (0,)), ((), ())), preferred_element_type=F32)


def _moe(x, hn, router_w, wg, wu, wd):
    t = x.shape[0]
    dffe = wg.shape[2]
    ts = min(1024, t)
    tg = min(256, t)
    te = min(1024, t)
    tf = 896 if dffe % 896 == 0 else 512
    assert t % ts == 0 and te % tg == 0 and dffe % tf == 0
    nb = t // ts
    n_eblk = (2 * t) // te + N_EXPERTS
    ns = n_eblk * te
    n_gblk = ns // tg
    n_items = n_gblk + N_EXPERTS * nb

    w_pad = jnp.zeros((D_MODEL, LANES), BF16).at[:, :N_EXPERTS].set(router_w.astype(BF16))
    tri = (jnp.arange(ts)[:, None] > jnp.arange(ts)[None, :]).astype(BF16)
    meta, cum, tot = pl.pallas_call(
        _router_kernel,
        grid=(nb,),
        in_specs=[pl.BlockSpec((ts, D_MODEL), lambda i: (i, 0)), pl.BlockSpec((D_MODEL, LANES), lambda i: (0, 0)),
                  pl.BlockSpec((ts, ts), lambda i: (0, 0))],
        out_specs=[pl.BlockSpec((1, ROUTER_FIELDS, ts), lambda i: (i, 0, 0)),
                   pl.BlockSpec((1, 1, LANES), lambda i: (i, 0, 0)), pl.BlockSpec((1, LANES), lambda i: (0, 0))],
        out_shape=[jax.ShapeDtypeStruct((nb, ROUTER_FIELDS, ts), F32), jax.ShapeDtypeStruct((nb, 1, LANES), F32),
                   jax.ShapeDtypeStruct((1, LANES), F32)],
        scratch_shapes=[pltpu.VMEM((1, LANES), F32)],
        compiler_params=_cparams("arbitrary"),
        name="moe_router",
    )(hn, w_pad, tri)

    e1 = meta[:, 0, :].astype(I32)
    e2 = meta[:, 1, :].astype(I32)
    g1 = meta[:, 2, :]
    g2 = meta[:, 3, :]
    r1 = meta[:, 4, :].astype(I32)
    r2 = meta[:, 5, :].astype(I32)
    counts = tot[0, :N_EXPERTS].astype(I32)
    cum_e = cum[:, 0, :N_EXPERTS].astype(I32)
    gpad = ((counts + te - 1) // te) * te
    gend = jnp.cumsum(gpad)
    gstart = gend - gpad

    def group_start(e):
        out = jnp.zeros(e.shape, I32)
        for ex in range(N_EXPERTS):
            out = jnp.where(e == ex, gstart[ex], out)
        return out

    pos0 = group_start(e1) + r1
    pos1 = group_start(e2) + r2
    nvalid_e = (gend[-1] // te).astype(I32)
    blk_e = jnp.arange(n_eblk, dtype=I32) * te
    be = jnp.minimum(jnp.sum(gend[None, :] <= blk_e[:, None], axis=1), N_EXPERTS - 1).astype(I32)
    be = jnp.where(jnp.arange(n_eblk) < nvalid_e, be, be[jnp.maximum(nvalid_e - 1, 0)])
    gb0 = jnp.arange(n_gblk, dtype=I32) * tg
    gb_valid = gb0 < gend[-1]
    gb_e = be[gb0 // te]
    rank0 = gb0 - gstart[gb_e]
    cnt_e = counts[gb_e]
    rank_last = jnp.maximum(jnp.minimum(rank0 + tg, cnt_e) - 1, 0)
    cum_cols = cum_e[:, gb_e]
    lo = jnp.sum(cum_cols <= jnp.minimum(rank0, rank_last)[None, :], axis=0) - 1
    hi = jnp.sum(cum_cols <= rank_last[None, :], axis=0) - 1
    lo = jnp.where(gb_valid, lo, 0)
    n_it = jnp.where(gb_valid, hi - lo + 1, 1)
    it_end = jnp.cumsum(n_it)
    it_start = it_end - n_it
    total = it_end[-1]
    ii = jnp.arange(n_items, dtype=I32)
    live = ii < total
    iic = jnp.minimum(ii, total - 1)
    it_j = jnp.sum(it_end[None, :] <= iic[:, None], axis=1).astype(I32)
    it_s = (lo[it_j] + iic - it_start[it_j]).astype(I32)
    it_first = (iic == it_start[it_j]) & live
    comp = live & gb_valid[it_j]
    it_fl = (comp.astype(I32) + 2 * it_first.astype(I32)).astype(I32)
    order = jnp.argsort(jnp.where(comp, it_s, nb), stable=True)
    cj = it_j[order]
    cs = it_s[order]
    cl = comp[order]
    last = jnp.maximum(jnp.sum(comp.astype(I32)) - 1, 0)
    cs = jnp.where(cl, cs, cs[last])
    cj = jnp.where(cl, cj, cj[last])
    c_first = cl & jnp.concatenate([jnp.ones((1,), bool), cs[1:] != cs[:-1]])
    c_fl = (cl.astype(I32) + 2 * c_first.astype(I32)).astype(I32)

    row3 = lambda v: v.reshape(nb, 1, ts)

    gspec = pltpu.PrefetchScalarGridSpec(
        num_scalar_prefetch=3,
        grid=(n_items,),
        in_specs=[pl.BlockSpec((ts, D_MODEL), lambda i, ij, is_, fl: (is_[i], 0))]
        + [pl.BlockSpec((1, 1, ts), lambda i, ij, is_, fl: (is_[i], 0, 0)) for _ in range(4)],
        out_specs=[pl.BlockSpec((tg, D_MODEL), lambda i, ij, is_, fl: (ij[i], 0)),
                   pl.BlockSpec((tg, 1), lambda i, ij, is_, fl: (ij[i], 0))],
    )
    xs, gs = pl.pallas_call(
        functools.partial(_gather_kernel, tg=tg, ts=ts),
        grid_spec=gspec,
        out_shape=[jax.ShapeDtypeStruct((ns, D_MODEL), BF16), jax.ShapeDtypeStruct((ns, 1), F32)],
        compiler_params=_cparams("arbitrary"),
        name="moe_gather",
    )(it_j, it_s, it_fl, hn, row3(pos0), row3(pos1), row3(g1), row3(g2))

    espec = pltpu.PrefetchScalarGridSpec(
        num_scalar_prefetch=2,
        grid=(n_eblk, dffe // tf),
        in_specs=[pl.BlockSpec((te, D_MODEL), lambda j, f, be_, nv: (j, 0)),
                  pl.BlockSpec((te, 1), lambda j, f, be_, nv: (j, 0)),
                  pl.BlockSpec((None, D_MODEL, tf), lambda j, f, be_, nv: (be_[j], 0, f)),
                  pl.BlockSpec((None, D_MODEL, tf), lambda j, f, be_, nv: (be_[j], 0, f)),
                  pl.BlockSpec((None, tf, D_MODEL), lambda j, f, be_, nv: (be_[j], f, 0))],
        out_specs=pl.BlockSpec((te, D_MODEL), lambda j, f, be_, nv: (j, 0)),
        scratch_shapes=[pltpu.VMEM((te, D_MODEL), F32)],
    )
    y = pl.pallas_call(
        _expert_kernel,
        grid_spec=espec,
        out_shape=jax.ShapeDtypeStruct((ns, D_MODEL), BF16),
        compiler_params=_cparams("parallel", "arbitrary"),
        name="moe_experts",
    )(be, nvalid_e.reshape(1), xs, gs, wg, wu, wd)

    cspec = pltpu.PrefetchScalarGridSpec(
        num_scalar_prefetch=3,
        grid=(n_items,),
        in_specs=[pl.BlockSpec((ts, D_MODEL), lambda i, cj_, cs_, fl: (cs_[i], 0)),
                  pl.BlockSpec((tg, D_MODEL), lambda i, cj_, cs_, fl: (cj_[i], 0)),
                  pl.BlockSpec((1, 1, ts), lambda i, cj_, cs_, fl: (cs_[i], 0, 0)),
                  pl.BlockSpec((1, 1, ts), lambda i, cj_, cs_, fl: (cs_[i], 0, 0))],
        out_specs=pl.BlockSpec((ts, D_MODEL), lambda i, cj_, cs_, fl: (cs_[i], 0)),
    )
    return pl.pallas_call(
        functools.partial(_combine_kernel, tg=tg, ts=ts),
        grid_spec=cspec,
        out_shape=jax.ShapeDtypeStruct((t, D_MODEL), F32),
        compiler_params=_cparams("arbitrary"),
        name="moe_combine",
    )(cj, cs, c_fl, x, y, row3(pos0), row3(pos1))


def _layer(x, b, l, li, p, mem_k, mem_v, past, kv_stack=None):
    h = _rms(x, p["norm1"])
    w_in = p["w_in"]
    seg = lambda a: w_in[:, SPLITS[a]:(SPLITS[a + 1] if a + 1 < len(SPLITS) else None)]
    tile = lambda g, n: jnp.tile(g.astype(F32), n)
    (q,) = _mm(h, seg(0), epi="seg64", gain=tile(p["q_norm_a"], 2 * H_A) * (DH_A ** -0.5 * LOG2E),
               layouts=("head",), bl=(b, l))
    k_stack = None if kv_stack is None else (li, kv_stack[0], kv_stack[1])
    v_stack = None if kv_stack is None else (li, kv_stack[0], kv_stack[2])
    tok = "tok" if kv_stack is None else "stack"
    k32, kbf = _mm(h, seg(1), epi="seg64", gain=tile(p["k_norm_a"], 2 * H_A), out_dtypes=(F32, BF16),
                   layouts=(tok, "head"), bl=(b, l), stack=k_stack)
    v32, vbf = _mm(h, seg(2), out_dtypes=(F32, BF16), layouts=(tok, "head"), bl=(b, l), stack=v_stack)
    (u_tm,) = _mm(h, seg(3), out_dtypes=(F32,), layouts=("time",), bl=(b, l))
    (qm,) = _mm(h, _pad_heads(seg(4)), epi="pad192", gain=_pad_heads(tile(p["q_norm_m"], H_M) * (DH_M ** -0.5)))
    (gates,) = _mm(h, seg(5), epi="sigmoid", tn=1024)

    lam_init = 0.8 - 0.6 * math.exp(-0.3 * li)
    lam = (jnp.exp(jnp.sum(p["lam_q1"].astype(F32) * p["lam_k1"].astype(F32)))
           - jnp.exp(jnp.sum(p["lam_q2"].astype(F32) * p["lam_k2"].astype(F32))) + lam_init)
    if past is None:
        o_a = _attn_prompt(q, kbf, vbf, p["rel_bias"], p["subln_a"], lam, lam_init, b, l)
        zero = jnp.zeros((b, N_GROUPS, STATE_P), F32)
        y_b, s_re, s_im = _s5(u_tm, p, zero, zero, b, l)
    else:
        cache_k, cache_v, s0_re, s0_im = past
        o_a = _attn_sample(q, kbf, vbf, cache_k, cache_v, li, p["rel_bias"], p["subln_a"], lam, lam_init, b, l)
        y_b, s_re, s_im = _s5(u_tm, p, s0_re, s0_im, b, l)
    o_m = _memattn(qm, mem_k, mem_v, b, l)
    w_om_pad = _pad_heads(p["w_om"].T).T
    x, hn = _merge(x, o_a, y_b, o_m, gates, p["w_oa"], p["w_ob"], w_om_pad, p["w_out"], p["norm2"], b, l)
    if li % 2 == 0:
        x = _ffn(x, hn, p["ffn_w_gate"], p["ffn_w_up"], p["ffn_w_down"])
    else:
        x = _moe(x, hn, p["router"], p["moe_w_gate"], p["moe_w_up"], p["moe_w_down"])
    return x, k32, v32, s_re, s_im


def kernel(x_prompt, x_sample, mem_prompt, cache_attn_k, cache_attn_v, state_ssm_re, state_ssm_im, cache_mem_k, cache_mem_v, norm1, norm2, w_in, q_norm_a, k_norm_a, lam_q1, lam_k1, lam_q2, lam_k2, subln_a, w_oa, lambda_re, lambda_im, log_dt, b_re, b_im, c_re, c_im, d_skip, w_glu, b_glu, w_ob, w_mk, w_mv, q_norm_m, k_norm_m, w_om, w_out, rel_bias, ffn_w_gate, ffn_w_up, ffn_w_down, router, moe_w_gate, moe_w_up, moe_w_down):
    depth = w_in.shape[0]
    bp, lp, _ = x_prompt.shape
    bs, ls, _ = x_sample.shape
    n_mem = mem_prompt.shape[1]
    xp = x_prompt.reshape(bp * lp, D_MODEL)
    xs = x_sample.reshape(bs * ls, D_MODEL)
    mem_bf = mem_prompt.reshape(bp * n_mem, D_MODEL).astype(BF16)
    outs = {name: [] for name in ("srp", "sip", "mkp", "mvp", "ks", "vs", "srs", "sis")}
    kp_buf = vp_buf = None
    for li in range(depth):
        j = li // 2
        p = {
            "norm1": norm1[li], "norm2": norm2[li], "w_in": w_in[li].astype(BF16),
            "q_norm_a": q_norm_a[li], "k_norm_a": k_norm_a[li],
            "lam_q1": lam_q1[li], "lam_k1": lam_k1[li], "lam_q2": lam_q2[li], "lam_k2": lam_k2[li],
            "subln_a": subln_a[li], "w_oa": w_oa[li].astype(BF16),
            "lambda_re": lambda_re[li], "lambda_im": lambda_im[li], "log_dt": log_dt[li],
            "b_re": b_re[li], "b_im": b_im[li], "c_re": c_re[li], "c_im": c_im[li],
            "d_skip": d_skip[li], "w_glu": w_glu[li], "b_glu": b_glu[li], "w_ob": w_ob[li].astype(BF16),
            "q_norm_m": q_norm_m[li], "w_om": w_om[li].astype(BF16), "w_out": w_out[li].astype(BF16),
            "rel_bias": rel_bias,
        }
        if li % 2 == 0:
            p.update(ffn_w_gate=ffn_w_gate[j].astype(BF16), ffn_w_up=ffn_w_up[j].astype(BF16),
                     ffn_w_down=ffn_w_down[j].astype(BF16))
        else:
            p.update(router=router[j], moe_w_gate=moe_w_gate[j].astype(BF16), moe_w_up=moe_w_up[j].astype(BF16),
                     moe_w_down=moe_w_down[j].astype(BF16))
        mk32, mkbf = _mm(mem_bf, w_mk[li].astype(BF16), epi="seg192", gain=jnp.tile(k_norm_m[li].astype(F32), H_M),
                         out_dtypes=(F32, BF16))
        mv32, mvbf = _mm(mem_bf, w_mv[li].astype(BF16), out_dtypes=(F32, BF16))
        xp, kp_buf, vp_buf, sr_p, si_p = _layer(xp, bp, lp, li, p, _pad_heads(mkbf.reshape(bp, n_mem, W_M)),
                                                _pad_heads(mvbf.reshape(bp, n_mem, W_M)), None,
                                                kv_stack=(depth, kp_buf, vp_buf))
        past = (cache_attn_k, cache_attn_v, state_ssm_re[li], state_ssm_im[li])
        xs, k_s, v_s, sr_s, si_s = _layer(xs, bs, ls, li, p,
                                          _pad_heads(cache_mem_k[li].reshape(bs, n_mem, W_M).astype(BF16)),
                                          _pad_heads(cache_mem_v[li].reshape(bs, n_mem, W_M).astype(BF16)), past)
        outs["srp"].append(sr_p)
        outs["sip"].append(si_p)
        outs["mkp"].append(mk32.reshape(bp, n_mem, H_M, DH_M))
        outs["mvp"].append(mv32.reshape(bp, n_mem, H_M, DH_M))
        outs["ks"].append(k_s.reshape(bs, ls, H_A, DV_A))
        outs["vs"].append(v_s.reshape(bs, ls, H_A, DV_A))
        outs["srs"].append(sr_s)
        outs["sis"].append(si_s)
    st = lambda name: jnp.stack(outs[name])
    kv_shape = (depth, bp, lp, H_A, DV_A)
    return (xp.reshape(bp, lp, D_MODEL), xs.reshape(bs, ls, D_MODEL),
            kp_buf.reshape(kv_shape), vp_buf.reshape(kv_shape), st("srp"), st("sip"),
            st("mkp"), st("mvp"), st("ks"), st("vs"), st("srs"), st("sis"))
```

```python
import functools
import math

import jax
import jax.numpy as jnp
from jax import lax
from jax.experimental import pallas as pl
from jax.experimental.pallas import tpu as pltpu

F32 = jnp.float32
BF16 = jnp.bfloat16
I32 = jnp.int32

D_MODEL = 1024
CHUNK = 64
H_A = 8
DH_A = 64
DV_A = 2 * DH_A
W_A = H_A * DV_A
GROUP_CH = 16
N_GROUPS = 48
STATE_P = 64
W_B = N_GROUPS * GROUP_CH
H_M = 4
DH_M = 192
W_M = H_M * DH_M
DH_M_PAD = 256
W_M_PAD = H_M * DH_M_PAD
REL_BUCKETS = 32
REL_MAX_DIST = 128
N_EXPERTS = 8
ROUTER_FIELDS = 8
EPS = 1e-6
NEG_INF = -1e30
LOG2E = math.log2(math.e)
Q_A_W = H_A * 2 * DH_A
SPLITS = (0, Q_A_W, 2 * Q_A_W, 2 * Q_A_W + W_A, 2 * Q_A_W + W_A + W_B, 2 * Q_A_W + W_A + W_B + W_M)
GATE_W = 3 * D_MODEL

LANES = 128
VMEM_LIMIT = 56 * 1024 * 1024
ATTN_GROUP = 4
S5_SLAB_GROUPS = LANES // GROUP_CH
S5_SLABS = N_GROUPS // S5_SLAB_GROUPS
S5_SLAB_STATE = S5_SLAB_GROUPS * STATE_P


def _cparams(*sem):
    return pltpu.CompilerParams(dimension_semantics=sem, vmem_limit_bytes=VMEM_LIMIT)


def _rms_kernel(x_ref, g_ref, o_ref):
    x = x_ref[...]
    ms = jnp.mean(x * x, axis=-1, keepdims=True)
    o_ref[...] = ((x * lax.rsqrt(ms + EPS)) * g_ref[...]).astype(o_ref.dtype)


def _rms(x, g):
    t, d = x.shape
    tm = min(1024, t)
    return pl.pallas_call(
        _rms_kernel,
        grid=(t // tm,),
        in_specs=[pl.BlockSpec((tm, d), lambda i: (i, 0)), pl.BlockSpec((1, d), lambda i: (0, 0))],
        out_specs=pl.BlockSpec((tm, d), lambda i: (i, 0)),
        out_shape=jax.ShapeDtypeStruct((t, d), BF16),
        compiler_params=_cparams("parallel"),
        name="rmsnorm",
    )(x, g.reshape(1, d).astype(F32))


def _seg64_scale(ys):
    lane = lax.broadcasted_iota(I32, (1, LANES), 1)
    left = lane < DH_A
    y2 = ys * ys
    sl = jnp.sum(jnp.where(left, y2, 0.0), axis=-1, keepdims=True)
    sr = jnp.sum(jnp.where(left, 0.0, y2), axis=-1, keepdims=True)
    rl = lax.rsqrt(sl * (1.0 / DH_A) + EPS)
    rr = lax.rsqrt(sr * (1.0 / DH_A) + EPS)
    return jnp.where(left, rl, rr)


def _seg192_scale(y):
    col = lax.broadcasted_iota(I32, (1, W_M), 1)
    y2 = y * y
    rb = jnp.zeros_like(y)
    for h in range(H_M):
        m = (col >= DH_M * h) & (col < DH_M * (h + 1))
        s = jnp.sum(jnp.where(m, y2, 0.0), axis=-1, keepdims=True)
        rb = jnp.where(m, lax.rsqrt(s / DH_M + EPS), rb)
    return rb


def _mm_kernel(*refs, epi, layouts, has_gain, fill_li=None):
    a_ref, w_ref = refs[0], refs[1]
    g_ref = refs[2] if has_gain else None
    outs = list(refs[len(refs) - len(layouts):])
    for k, lay in enumerate(layouts):
        if lay == "stack_fill":
            o = outs[k]
            for d in range(o.shape[0]):
                if d != fill_li:
                    o[d] = jnp.zeros(o.shape[1:], o.dtype)
            outs[k] = o.at[fill_li]
    y = jnp.dot(a_ref[...], w_ref[...], preferred_element_type=F32)
    tn = y.shape[1]
    if epi == "seg192":
        y = (y * _seg192_scale(y)) * g_ref[...]
    elif epi == "pad192":
        for h in range(tn // DH_M_PAD):
            sl = slice(h * DH_M_PAD, (h + 1) * DH_M_PAD)
            ys = y[:, sl]
            r = lax.rsqrt(jnp.sum(ys * ys, axis=-1, keepdims=True) / DH_M + EPS)
            for o in outs:
                o[:, sl] = ((ys * r) * g_ref[:, sl]).astype(o.dtype)
        return
    elif epi == "sigmoid":
        y = jax.nn.sigmoid(y)
    if epi == "seg64" or "head" in layouts:
        for s in range(tn // LANES):
            sl = slice(s * LANES, (s + 1) * LANES)
            ys = y[:, sl]
            if epi == "seg64":
                ys = (ys * _seg64_scale(ys)) * g_ref[:, sl]
            for o, lay in zip(outs, layouts):
                if lay == "head":
                    o[s] = ys.astype(o.dtype)
                else:
                    o[:, sl] = ys.astype(o.dtype)
        return
    for o in outs:
        o[...] = y.astype(o.dtype)


def _mm(a, w, *, epi="plain", gain=None, out_dtypes=(BF16,), layouts=None, tn=None, bl=None, stack=None):
    t, k = a.shape
    n = w.shape[1]
    tn = n if tn is None else tn
    layouts = ("tok",) * len(out_dtypes) if layouts is None else layouts
    assert n % tn == 0 and tn % LANES == 0
    if bl is None:
        assert all(lay in ("tok", "stack") for lay in layouts)
        tm, nl = min(1024, t), 1
    else:
        b, l = bl
        tm = min(1024, l)
        assert l % tm == 0
        nl = l // tm
    assert t % tm == 0
    out_specs, out_shapes = [], []
    alias_args, aliases = [], {}
    n_in = 2 + int(gain is not None)
    for oi, (dt, lay) in enumerate(zip(out_dtypes, layouts)):
        if lay == "tok":
            out_specs.append(pl.BlockSpec((tm, tn), lambda i, j: (i, j)))
            out_shapes.append(jax.ShapeDtypeStruct((t, n), dt))
        elif lay == "stack":
            li, depth, buf = stack
            out_shapes.append(jax.ShapeDtypeStruct((depth, t, n), dt))
            if buf is None:
                out_specs.append(pl.BlockSpec((depth, tm, tn), lambda i, j: (0, i, j)))
                layouts = tuple("stack_fill" if k == oi else lay_k for k, lay_k in enumerate(layouts))
            else:
                out_specs.append(pl.BlockSpec((None, tm, tn), lambda i, j: (li, i, j)))
                aliases[n_in + len(alias_args)] = oi
                alias_args.append(buf)
        elif lay == "time":
            assert tn == n
            out_specs.append(pl.BlockSpec((tm, tn), lambda i, j: (i % nl, i // nl)))
            out_shapes.append(jax.ShapeDtypeStruct((l, b * n), dt))
        else:
            out_specs.append(pl.BlockSpec((None, tn // LANES, tm, LANES), lambda i, j: (i // nl, j, i % nl, 0)))
            out_shapes.append(jax.ShapeDtypeStruct((b, n // LANES, l, LANES), dt))
    in_specs = [pl.BlockSpec((tm, k), lambda i, j: (i, 0)), pl.BlockSpec((k, tn), lambda i, j: (0, j))]
    args = [a, w]
    if gain is not None:
        in_specs.append(pl.BlockSpec((1, tn), lambda i, j: (0, j)))
        args.append(gain.reshape(1, n).astype(F32))
    in_specs += [pl.BlockSpec(memory_space=pl.ANY) for _ in alias_args]
    return pl.pallas_call(
        functools.partial(_mm_kernel, epi=epi, layouts=tuple(layouts), has_gain=gain is not None,
                          fill_li=None if stack is None else stack[0]),
        grid=(t // tm, n // tn),
        in_specs=in_specs,
        out_specs=out_specs,
        out_shape=out_shapes,
        input_output_aliases=aliases,
        compiler_params=_cparams("parallel", "parallel"),
        name="mm_" + epi,
    )(*args, *alias_args)


def _rel_bias(q_pos, k_pos, table):
    rel = k_pos[None, :] - q_pos[:, None]
    half = REL_BUCKETS // 2
    max_exact = half // 2
    n = jnp.abs(rel)
    nf = jnp.maximum(n, 1).astype(F32)
    large = max_exact + (jnp.log(nf / max_exact) / math.log(REL_MAX_DIST / max_exact)
                         * (half - max_exact)).astype(I32)
    large = jnp.minimum(large, half - 1)
    bucket = jnp.where(rel > 0, half, 0) + jnp.where(n < max_exact, n, large)
    tab = table.astype(F32)
    bias = jnp.zeros((H_A,) + bucket.shape, F32)
    for bk in range(REL_BUCKETS):
        bias = jnp.where((bucket == bk)[None], tab[bk][:, None, None], bias)
    visible = (k_pos[None, :] // CHUNK) <= (q_pos[:, None] // CHUNK)
    return jnp.where(visible[None], bias, NEG_INF)


def _split_halves(q):
    lane = lax.broadcasted_iota(I32, (1, LANES), 1)
    zero = jnp.zeros_like(q)
    return jnp.concatenate([jnp.where(lane < DH_A, q, zero), jnp.where(lane >= DH_A, q, zero)], axis=0)


def _subln(o, g, lam_init):
    ms = jnp.mean(o * o, axis=-1, keepdims=True)
    return ((o * lax.rsqrt(ms + EPS)) * g) * (1.0 - lam_init)


def _attn_prompt_kernel(lam_ref, q_ref, k_ref, v_ref, bias_ref, g_ref, o_ref,
                        s_sc, m_sc, acc_sc, *, tq, lam_init):
    qi = pl.program_id(2)
    n_blk = qi + 1
    qs = _split_halves(q_ref[...])
    nt = (((1,), (1,)), ((), ()))
    nc = tq // LANES

    def fold(op, s):
        r = s[:, :LANES]
        for c in range(1, nc):
            r = op(r, s[:, c * LANES:(c + 1) * LANES])
        return r

    def score(j):
        r0 = pl.multiple_of(j * tq, tq)
        bias = bias_ref[jnp.clip(j - (qi - 2), 0, 2)]
        s = lax.dot_general(qs, k_ref[pl.ds(r0, tq), :], nt, preferred_element_type=F32)
        s = s + jnp.concatenate([bias, bias], axis=0)
        s_sc[j] = s
        return fold(jnp.maximum, s)

    def sweep(group_fn):
        n_full = n_blk // ATTN_GROUP

        def trip(g, c):
            group_fn(ATTN_GROUP * g, ATTN_GROUP)
            return c

        lax.fori_loop(0, n_full, trip, 0)
        rem = n_blk - ATTN_GROUP * n_full

        @pl.when(rem >= 2)
        def _():
            group_fn(ATTN_GROUP * n_full, 2)

        @pl.when(rem % 2 == 1)
        def _():
            group_fn(qi, 1)

    m_sc[...] = jnp.full(m_sc.shape, -jnp.inf, F32)

    def score_group(j0, g):
        m_sc[...] = jnp.maximum(m_sc[...], functools.reduce(jnp.maximum, [score(j0 + t) for t in range(g)]))

    sweep(score_group)
    m = jnp.max(m_sc[...], axis=-1, keepdims=True)
    m_sc[...] = jnp.broadcast_to(m, m_sc.shape)
    acc_sc[...] = jnp.zeros(acc_sc.shape, F32)
    ones = jnp.ones((tq, DV_A), BF16)

    def pv(j):
        mrow = m_sc[...]
        s = s_sc[j]
        p = jnp.concatenate([jnp.exp2(s[:, cc * LANES:(cc + 1) * LANES] - mrow) for cc in range(nc)], axis=1)
        r0 = pl.multiple_of(j * tq, tq)
        v_ext = jnp.concatenate([v_ref[pl.ds(r0, tq), :], ones], axis=1)
        return jnp.dot(p.astype(BF16), v_ext, preferred_element_type=F32)

    def pv_group(j0, g):
        acc_sc[...] += functools.reduce(jnp.add, [pv(j0 + t) for t in range(g)])

    sweep(pv_group)
    acc = acc_sc[...]
    o = acc[:, :DV_A] / acc[:, DV_A:]
    o = o[:tq] - lam_ref[0] * o[tq:]
    o_ref[...] = _subln(o, g_ref[...], lam_init).astype(o_ref.dtype)


def _attn_prompt(q, k, v, rel_table, subln_g, lam, lam_init, b, l):
    tq = min(512, l)
    assert l % tq == 0 and tq % CHUNK == 0 and tq >= 128
    cfar = _rel_bias(jnp.array([2 * tq + 1]), jnp.array([0]), rel_table)[:, 0, 0]
    bias = _rel_bias(jnp.arange(tq, 2 * tq), jnp.arange(2 * tq), rel_table)
    bias = (bias - cfar[:, None, None]) * LOG2E
    bias = jnp.stack([jnp.zeros_like(bias[:, :, :tq]), bias[:, :, :tq], bias[:, :, tq:]], axis=1)
    smem = pl.BlockSpec(memory_space=pltpu.SMEM)
    out = pl.pallas_call(
        functools.partial(_attn_prompt_kernel, tq=tq, lam_init=lam_init),
        grid=(b, H_A, l // tq),
        in_specs=[smem,
                  pl.BlockSpec((None, None, tq, DV_A), lambda bi, h, i: (bi, h, i, 0)),
                  pl.BlockSpec((None, None, l, DV_A), lambda bi, h, i: (bi, h, 0, 0)),
                  pl.BlockSpec((None, None, l, DV_A), lambda bi, h, i: (bi, h, 0, 0)),
                  pl.BlockSpec((None, 3, tq, tq), lambda bi, h, i: (h, 0, 0, 0)),
                  pl.BlockSpec((1, DV_A), lambda bi, h, i: (0, 0))],
        out_specs=pl.BlockSpec((None, tq, DV_A), lambda bi, h, i: (bi, i, h)),
        out_shape=jax.ShapeDtypeStruct((b, l, W_A), BF16),
        scratch_shapes=[pltpu.VMEM((l // tq, 2 * tq, tq), F32), pltpu.VMEM((2 * tq, LANES), F32),
                        pltpu.VMEM((2 * tq, 2 * DV_A), F32)],
        compiler_params=_cparams("parallel", "parallel", "arbitrary"),
        name="attn_prompt",
    )(lam.reshape(1), q, k, v, bias, subln_g.reshape(1, DV_A).astype(F32))
    return out.reshape(b * l, W_A)


def _attn_sample_kernel(lam_ref, q_ref, kp_ref, vp_ref, kn_ref, vn_ref, bp_ref, bn_ref, g_ref, o_ref,
                        *, lq, lam_init):
    lam = lam_ref[0]
    nt = (((1,), (1,)), ((), ()))
    for h in range(H_A):
        sl = slice(h * DV_A, (h + 1) * DV_A)
        qs = _split_halves(q_ref[h])
        kp = kp_ref[:, h, :].astype(BF16)
        vp = vp_ref[:, h, :].astype(BF16)
        bp = bp_ref[h]
        bn = bn_ref[h]
        sp = lax.dot_general(qs, kp, nt, preferred_element_type=F32) + jnp.concatenate([bp, bp], axis=0)
        sn = lax.dot_general(qs, kn_ref[h], nt, preferred_element_type=F32) + jnp.concatenate([bn, bn], axis=0)
        m = jnp.maximum(jnp.max(sp, axis=-1, keepdims=True), jnp.max(sn, axis=-1, keepdims=True))
        pp = jnp.exp2(sp - m)
        pn = jnp.exp2(sn - m)
        lsum = jnp.sum(pp, axis=-1, keepdims=True) + jnp.sum(pn, axis=-1, keepdims=True)
        o = (jnp.dot(pp.astype(BF16), vp, preferred_element_type=F32)
             + jnp.dot(pn.astype(BF16), vn_ref[h], preferred_element_type=F32)) / lsum
        o = o[:lq] - lam * o[lq:]
        o_ref[:, sl] = _subln(o, g_ref[...], lam_init).astype(o_ref.dtype)


def _attn_sample(q, k_new, v_new, cache_k, cache_v, li, rel_table, subln_g, lam, lam_init, b, l):
    past = cache_k.shape[2]
    k_pos = jnp.arange(past + l)
    bias = _rel_bias(k_pos[past:], k_pos, rel_table) * LOG2E
    bp, bn = bias[:, :, :past], bias[:, :, past:]
    smem = pl.BlockSpec(memory_space=pltpu.SMEM)
    tok = lambda rows: pl.BlockSpec((None, rows, W_A), lambda bi: (bi, 0, 0))
    head = pl.BlockSpec((None, H_A, l, DV_A), lambda bi: (bi, 0, 0, 0))
    cache = pl.BlockSpec((None, None, past, H_A, DV_A), lambda bi: (li, bi, 0, 0, 0))
    full = lambda shape: pl.BlockSpec(shape, lambda bi: tuple(0 for _ in shape))
    out = pl.pallas_call(
        functools.partial(_attn_sample_kernel, lq=l, lam_init=lam_init),
        grid=(b,),
        in_specs=[smem, head, cache, cache, head, head,
                  full((H_A, l, past)), full((H_A, l, l)), full((1, DV_A))],
        out_specs=tok(l),
        out_shape=jax.ShapeDtypeStruct((b, l, W_A), BF16),
        compiler_params=_cparams("parallel"),
        name="attn_sample",
    )(lam.reshape(1), q, cache_k, cache_v, k_new, v_new, bp, bn, subln_g.reshape(1, DV_A).astype(F32))
    return out.reshape(b * l, W_A)


def _pad_heads(z):
    lead = z.shape[:-1]
    z = z.reshape(*lead, H_M, DH_M)
    z = jnp.pad(z, [(0, 0)] * len(lead) + [(0, 0), (0, DH_M_PAD - DH_M)])
    return z.reshape(*lead, W_M_PAD)


def _memattn_kernel(q_ref, mk_ref, mv_ref, o_ref):
    nt = (((1,), (1,)), ((), ()))
    for h in range(H_M):
        sl = slice(h * DH_M_PAD, (h + 1) * DH_M_PAD)
        s = lax.dot_general(q_ref[:, sl], mk_ref[:, sl], nt, preferred_element_type=F32)
        m = jnp.max(s, axis=-1, keepdims=True)
        p = jnp.exp(s - m)
        p = p / jnp.sum(p, axis=-1, keepdims=True)
        o_ref[:, sl] = jnp.dot(p.astype(BF16), mv_ref[:, sl], preferred_element_type=F32).astype(o_ref.dtype)


def _memattn(qm, mk, mv, b, l):
    n_mem = mk.shape[1]
    tq = min(512, l)
    assert l % tq == 0
    out = pl.pallas_call(
        _memattn_kernel,
        grid=(b, l // tq),
        in_specs=[pl.BlockSpec((None, tq, W_M_PAD), lambda bi, i: (bi, i, 0)),
                  pl.BlockSpec((None, n_mem, W_M_PAD), lambda bi, i: (bi, 0, 0)),
                  pl.BlockSpec((None, n_mem, W_M_PAD), lambda bi, i: (bi, 0, 0))],
        out_specs=pl.BlockSpec((None, tq, W_M_PAD), lambda bi, i: (bi, i, 0)),
        out_shape=jax.ShapeDtypeStruct((b, l, W_M_PAD), BF16),
        compiler_params=_cparams("parallel", "parallel"),
        name="memattn",
    )(qm.reshape(b, l, W_M_PAD), mk, mv)
    return out.reshape(b * l, W_M_PAD)


def _s5_tables(p):
    dt = jnp.exp(p["log_dt"].astype(F32))[:, None]
    lr = jnp.minimum(p["lambda_re"].astype(F32), -1e-4)
    lim = p["lambda_im"].astype(F32)
    mag = jnp.exp(lr * dt)
    ar = mag * jnp.cos(lim * dt)
    ai = mag * jnp.sin(lim * dt)
    den = lr * lr + lim * lim
    fr = ((ar - 1.0) * lr + ai * lim) / den
    fi = (ai * lr - (ar - 1.0) * lim) / den
    br = p["b_re"].astype(F32)
    bi = p["b_im"].astype(F32)
    bbr = fr[..., None] * br - fi[..., None] * bi
    bbi = fr[..., None] * bi + fi[..., None] * br
    eye = jnp.eye(S5_SLAB_GROUPS, dtype=F32)
    sg = (S5_SLABS, S5_SLAB_GROUPS)

    def in_w(bb):
        w = jnp.einsum("sgpc,gh->sgchp", bb.reshape(*sg, STATE_P, GROUP_CH), eye)
        return w.reshape(S5_SLABS, LANES, S5_SLAB_STATE)

    def out_w(c):
        w = jnp.einsum("sgcp,gh->sgphc", c.reshape(*sg, GROUP_CH, STATE_P), eye)
        return w.reshape(S5_SLABS, S5_SLAB_STATE, LANES)

    wb = jnp.concatenate([in_w(bbr), in_w(bbi)], axis=2).astype(BF16)
    wc = jnp.concatenate([out_w(p["c_re"].astype(F32)), -out_w(p["c_im"].astype(F32))], axis=1).astype(BF16)
    a_re = ar.reshape(S5_SLABS, 1, S5_SLAB_STATE)
    a_im = ai.reshape(S5_SLABS, 1, S5_SLAB_STATE)
    return wb, wc, a_re, a_im


def _s5_kernel(u_ref, wb_ref, wc_ref, ar_ref, ai_ref, d_ref, wglu_ref, bglu_ref, s0r_ref, s0i_ref,
               y_ref, sr_ref, si_ref, bu_sc, x_sc, y_sc, *, tl, nb):
    i = pl.program_id(0)

    @pl.when(i == 0)
    def _():
        sr_ref[...] = s0r_ref[...]
        si_ref[...] = s0i_ref[...]

    rows = tl * nb
    u = u_ref[...].reshape(rows, W_B)
    y_sc[...] = d_ref[...] * u
    for s in range(S5_SLABS):
        sl = slice(s * LANES, (s + 1) * LANES)
        bu_sc[...] = jnp.dot(u[:, sl].astype(BF16), wb_ref[s], preferred_element_type=F32)
        ar = jnp.broadcast_to(ar_ref[s], (nb, S5_SLAB_STATE))
        ai = jnp.broadcast_to(ai_ref[s], (nb, S5_SLAB_STATE))

        def body(t, carry):
            sr, si = carry
            r0 = pl.multiple_of(t * nb, nb)
            b_r = bu_sc[pl.ds(r0, nb), :S5_SLAB_STATE]
            b_i = bu_sc[pl.ds(r0, nb), S5_SLAB_STATE:]
            nr = ar * sr - ai * si + b_r
            ni = ar * si + ai * sr + b_i
            x_sc[pl.ds(r0, nb), :S5_SLAB_STATE] = nr.astype(BF16)
            x_sc[pl.ds(r0, nb), S5_SLAB_STATE:] = ni.astype(BF16)
            return nr, ni

        sr, si = lax.fori_loop(0, tl, body, (sr_ref[s], si_ref[s]), unroll=True)
        sr_ref[s] = sr
        si_ref[s] = si
        y_sc[:, sl] = y_sc[:, sl] + jnp.dot(x_sc[...], wc_ref[s], preferred_element_type=F32)
    yg = jax.nn.gelu(y_sc[...])
    z = jnp.dot(yg.astype(BF16), wglu_ref[...], preferred_element_type=F32) + bglu_ref[...]
    out = yg * jax.nn.sigmoid(z)
    y_ref[...] = out.astype(y_ref.dtype).reshape(tl, nb, W_B)


def _s5(u_tm, p, s0_re, s0_im, b, l):
    assert b % 16 == 0
    tl = min(64, l)
    assert l % tl == 0
    wb, wc, a_re, a_im = _s5_tables(p)
    to_slab = lambda s0: jnp.transpose(s0.astype(F32).reshape(b, S5_SLABS, S5_SLAB_STATE), (1, 0, 2))
    from_slab = lambda st: jnp.transpose(st, (1, 0, 2)).reshape(b, N_GROUPS, STATE_P)
    full = lambda shape: pl.BlockSpec(shape, lambda i: tuple(0 for _ in shape))
    st_shape = (S5_SLABS, b, S5_SLAB_STATE)
    rows = tl * b
    y, sr, si = pl.pallas_call(
        functools.partial(_s5_kernel, tl=tl, nb=b),
        grid=(l // tl,),
        in_specs=[pl.BlockSpec((tl, b, W_B), lambda i: (i, 0, 0)),
                  full(wb.shape), full(wc.shape), full(a_re.shape), full(a_im.shape),
                  full((1, W_B)), full((W_B, W_B)), full((1, W_B)), full(st_shape), full(st_shape)],
        out_specs=[pl.BlockSpec((tl, b, W_B), lambda i: (i, 0, 0)), full(st_shape), full(st_shape)],
        out_shape=[jax.ShapeDtypeStruct((l, b, W_B), BF16),
                   jax.ShapeDtypeStruct(st_shape, F32), jax.ShapeDtypeStruct(st_shape, F32)],
        scratch_shapes=[pltpu.VMEM((rows, 2 * S5_SLAB_STATE), F32),
                        pltpu.VMEM((rows, 2 * S5_SLAB_STATE), BF16),
                        pltpu.VMEM((rows, W_B), F32)],
        compiler_params=_cparams("arbitrary"),
        name="s5_scan",
    )(u_tm.reshape(l, b, W_B), wb, wc, a_re, a_im,
      p["d_skip"].astype(F32).reshape(1, W_B), p["w_glu"].astype(BF16), p["b_glu"].astype(F32).reshape(1, W_B),
      to_slab(s0_re), to_slab(s0_im))
    return y.reshape(l, b * W_B), from_slab(sr), from_slab(si)


def _merge_kernel(x_ref, oa_ref, yb_ref, om_ref, gt_ref, woa_ref, wob_ref, wom_ref, wout_ref, g2_ref,
                  xo_ref, hn_ref):
    pa = jnp.dot(oa_ref[...], woa_ref[...], preferred_element_type=F32)
    pb = jnp.dot(yb_ref[...], wob_ref[...], preferred_element_type=F32)
    pm = jnp.dot(om_ref[...], wom_ref[...], preferred_element_type=F32)
    d = D_MODEL
    mix = (gt_ref[:, :d].astype(F32) * pa + gt_ref[:, d:2 * d].astype(F32) * pb
           + gt_ref[:, 2 * d:].astype(F32) * pm)
    x = x_ref[...] + jnp.dot(mix.astype(BF16), wout_ref[...], preferred_element_type=F32)
    xo_ref[...] = x
    ms = jnp.mean(x * x, axis=-1, keepdims=True)
    hn_ref[...] = ((x * lax.rsqrt(ms + EPS)) * g2_ref[...]).astype(hn_ref.dtype)


def _merge(x, o_a, y_b_tm, o_m, gates, w_oa, w_ob, w_om, w_out, g2, b, l):
    t = b * l
    tm = min(512, l)
    assert l % tm == 0
    nl = l // tm
    row = lambda w: pl.BlockSpec((tm, w), lambda bi, i: (bi * nl + i, 0))
    full = lambda shape: pl.BlockSpec(shape, lambda bi, i: (0, 0))
    return pl.pallas_call(
        _merge_kernel,
        grid=(b, nl),
        in_specs=[row(D_MODEL), row(W_A), pl.BlockSpec((tm, W_B), lambda bi, i: (i, bi)), row(W_M_PAD), row(GATE_W),
                  full(w_oa.shape), full(w_ob.shape), full(w_om.shape), full(w_out.shape), full((1, D_MODEL))],
        out_specs=[row(D_MODEL), row(D_MODEL)],
        out_shape=[jax.ShapeDtypeStruct((t, D_MODEL), F32), jax.ShapeDtypeStruct((t, D_MODEL), BF16)],
        compiler_params=_cparams("parallel", "parallel"),
        name="merge",
    )(x, o_a, y_b_tm, o_m, gates, w_oa, w_ob, w_om, w_out, g2.reshape(1, D_MODEL).astype(F32))


def _ffn_kernel(x_ref, h_ref, wg_ref, wu_ref, wd_ref, gn_ref, o_ref, hn_ref, acc_sc):
    f = pl.program_id(1)

    @pl.when(f == 0)
    def _():
        acc_sc[...] = x_ref[...]

    h = h_ref[...]
    hg = jnp.dot(h, wg_ref[...], preferred_element_type=F32)
    hu = jnp.dot(h, wu_ref[...], preferred_element_type=F32)
    mid = (jax.nn.silu(hg) * hu).astype(BF16)
    acc_sc[...] += jnp.dot(mid, wd_ref[...], preferred_element_type=F32)

    @pl.when(f == pl.num_programs(1) - 1)
    def _():
        y = acc_sc[...]
        o_ref[...] = y
        ms = jnp.mean(y * y, axis=-1, keepdims=True)
        hn_ref[...] = ((y * lax.rsqrt(ms + EPS)) * gn_ref[...]).astype(hn_ref.dtype)


def _ffn(x, hn, wg, wu, wd, next_gain):
    t = x.shape[0]
    dff = wg.shape[1]
    tm = min(512, t)
    tf = dff // 2 if (dff // 2) % LANES == 0 else dff
    return pl.pallas_call(
        _ffn_kernel,
        grid=(t // tm, dff // tf),
        in_specs=[pl.BlockSpec((tm, D_MODEL), lambda i, f: (i, 0)), pl.BlockSpec((tm, D_MODEL), lambda i, f: (i, 0)),
                  pl.BlockSpec((D_MODEL, tf), lambda i, f: (0, f)), pl.BlockSpec((D_MODEL, tf), lambda i, f: (0, f)),
                  pl.BlockSpec((tf, D_MODEL), lambda i, f: (f, 0)), pl.BlockSpec((1, D_MODEL), lambda i, f: (0, 0))],
        out_specs=[pl.BlockSpec((tm, D_MODEL), lambda i, f: (i, 0)), pl.BlockSpec((tm, D_MODEL), lambda i, f: (i, 0))],
        out_shape=[jax.ShapeDtypeStruct((t, D_MODEL), F32), jax.ShapeDtypeStruct((t, D_MODEL), BF16)],
        scratch_shapes=[pltpu.VMEM((tm, D_MODEL), F32)],
        compiler_params=_cparams("parallel", "arbitrary"),
        name="ffn_dense",
    )(x, hn, wg, wu, wd, next_gain.reshape(1, D_MODEL).astype(F32))


def _router_kernel(h_ref, w_ref, tri_ref, meta_ref, cum_ref, tot_ref, carry_sc):
    i = pl.program_id(0)

    @pl.when(i == 0)
    def _():
        carry_sc[...] = jnp.zeros(carry_sc.shape, F32)

    lane = lax.broadcasted_iota(I32, (1, LANES), 1)
    logits = jnp.dot(h_ref[...], w_ref[...], preferred_element_type=F32)
    lg = jnp.where(lane < N_EXPERTS, logits, -jnp.inf)
    m1 = jnp.max(lg, axis=-1, keepdims=True)
    i1 = jnp.min(jnp.where(lg == m1, lane, LANES), axis=-1, keepdims=True)
    lg2 = jnp.where(lane == i1, -jnp.inf, lg)
    m2 = jnp.max(lg2, axis=-1, keepdims=True)
    i2 = jnp.min(jnp.where(lg2 == m2, lane, LANES), axis=-1, keepdims=True)
    e = jnp.exp(m2 - m1)
    g1 = 1.0 / (1.0 + e)
    g2 = e / (1.0 + e)
    hit1 = lane == i1
    hit2 = lane == i2
    cnt = jnp.where(hit1 | hit2, 1.0, 0.0)
    carry = carry_sc[...]
    before = jnp.dot(tri_ref[...], cnt.astype(BF16), preferred_element_type=F32) + carry
    r1 = jnp.sum(jnp.where(hit1, before, 0.0), axis=-1, keepdims=True)
    r2 = jnp.sum(jnp.where(hit2, before, 0.0), axis=-1, keepdims=True)
    cum_ref[0] = carry
    carry = carry + jnp.sum(cnt, axis=0, keepdims=True)
    carry_sc[...] = carry
    tot_ref[...] = carry
    meta = jnp.zeros(logits.shape, F32)
    for c, val in enumerate((i1.astype(F32), i2.astype(F32), g1, g2, r1, r2)):
        meta = jnp.where(lane == c, val, meta)
    meta_ref[0] = meta.T[:ROUTER_FIELDS]


def _gather_kernel(ij_ref, is_ref, ifl_ref, h_ref, p0_ref, p1_ref, g0_ref, g1_ref, xs_ref, gs_ref, *, tg, ts):
    i = pl.program_id(0)
    fl = ifl_ref[i]

    @pl.when((fl & 2) != 0)
    def _():
        xs_ref[...] = jnp.zeros(xs_ref.shape, xs_ref.dtype)
        gs_ref[...] = jnp.zeros(gs_ref.shape, gs_ref.dtype)

    @pl.when((fl & 1) != 0)
    def _():
        rio = lax.broadcasted_iota(I32, (tg, ts), 0) + ij_ref[i] * tg
        m0 = p0_ref[0] == rio
        m1 = p1_ref[0] == rio
        sel = jnp.where(m0 | m1, 1.0, 0.0).astype(BF16)
        xs_ref[...] += jnp.dot(sel, h_ref[...], preferred_element_type=F32).astype(xs_ref.dtype)
        gs_ref[...] += jnp.sum(jnp.where(m0, g0_ref[0], 0.0) + jnp.where(m1, g1_ref[0], 0.0),
                               axis=-1, keepdims=True)


def _expert_kernel(be_ref, nv_ref, xs_ref, gs_ref, wg_ref, wu_ref, wd_ref, y_ref, acc_sc):
    j = pl.program_id(0)
    f = pl.program_id(1)
    valid = j < nv_ref[0]

    @pl.when(f == 0)
    def _():
        acc_sc[...] = jnp.zeros(acc_sc.shape, F32)

    @pl.when(valid)
    def _():
        x = xs_ref[...]
        hg = jnp.dot(x, wg_ref[...], preferred_element_type=F32)
        hu = jnp.dot(x, wu_ref[...], preferred_element_type=F32)
        mid = (jax.nn.silu(hg) * hu).astype(BF16)
        acc_sc[...] += jnp.dot(mid, wd_ref[...], preferred_element_type=F32)

    @pl.when(f == pl.num_programs(1) - 1)
    def _():
        y_ref[...] = jnp.where(valid, acc_sc[...] * gs_ref[...], 0.0).astype(y_ref.dtype)


def _combine_kernel(cj_ref, cs_ref, cfl_ref, x_ref, y_ref, p0_ref, p1_ref, o_ref, *, tg, ts):
    i = pl.program_id(0)
    fl = cfl_ref[i]

    @pl.when((fl & 2) != 0)
    def _():
        o_ref[...] = x_ref[...]

    @pl.when((fl & 1) != 0)
    def _():
        rio = lax.broadcasted_iota(I32, (tg, ts), 0) + cj_ref[i] * tg
        sel = jnp.where((p0_ref[0] == rio) | (p1_ref[0] == rio), 1.0, 0.0).astype(BF16)
        o_ref[...] += lax.dot_general(sel, y_ref[...], (((0,), (0,)), ((), ())), preferred_element_type=F32)


def _moe(x, hn, router_w, wg, wu, wd):
    t = x.shape[0]
    dffe = wg.shape[2]
    ts = min(1024, t)
    tg = min(256, t)
    te = min(1024, t)
    tf = 512
    assert t % ts == 0 and te % tg == 0 and dffe % tf == 0
    nb = t // ts
    n_eblk = (2 * t) // te + N_EXPERTS
    ns = n_eblk * te
    n_gblk = ns // tg
    n_items = n_gblk + N_EXPERTS * nb

    w_pad = jnp.zeros((D_MODEL, LANES), BF16).at[:, :N_EXPERTS].set(router_w.astype(BF16))
    tri = (jnp.arange(ts)[:, None] > jnp.arange(ts)[None, :]).astype(BF16)
    meta, cum, tot = pl.pallas_call(
        _router_kernel,
        grid=(nb,),
        in_specs=[pl.BlockSpec((ts, D_MODEL), lambda i: (i, 0)), pl.BlockSpec((D_MODEL, LANES), lambda i: (0, 0)),
                  pl.BlockSpec((ts, ts), lambda i: (0, 0))],
        out_specs=[pl.BlockSpec((1, ROUTER_FIELDS, ts), lambda i: (i, 0, 0)),
                   pl.BlockSpec((1, 1, LANES), lambda i: (i, 0, 0)), pl.BlockSpec((1, LANES), lambda i: (0, 0))],
        out_shape=[jax.ShapeDtypeStruct((nb, ROUTER_FIELDS, ts), F32), jax.ShapeDtypeStruct((nb, 1, LANES), F32),
                   jax.ShapeDtypeStruct((1, LANES), F32)],
        scratch_shapes=[pltpu.VMEM((1, LANES), F32)],
        compiler_params=_cparams("arbitrary"),
        name="moe_router",
    )(hn, w_pad, tri)

    e1 = meta[:, 0, :].astype(I32)
    e2 = meta[:, 1, :].astype(I32)
    g1 = meta[:, 2, :]
    g2 = meta[:, 3, :]
    r1 = meta[:, 4, :].astype(I32)
    r2 = meta[:, 5, :].astype(I32)
    counts = tot[0, :N_EXPERTS].astype(I32)
    cum_e = cum[:, 0, :N_EXPERTS].astype(I32)
    gpad = ((counts + te - 1) // te) * te
    gend = jnp.cumsum(gpad)
    gstart = gend - gpad

    def group_start(e):
        out = jnp.zeros(e.shape, I32)
        for ex in range(N_EXPERTS):
            out = jnp.where(e == ex, gstart[ex], out)
        return out

    pos0 = group_start(e1) + r1
    pos1 = group_start(e2) + r2
    nvalid_e = (gend[-1] // te).astype(I32)
    blk_e = jnp.arange(n_eblk, dtype=I32) * te
    be = jnp.minimum(jnp.sum(gend[None, :] <= blk_e[:, None], axis=1), N_EXPERTS - 1).astype(I32)
    be = jnp.where(jnp.arange(n_eblk) < nvalid_e, be, be[jnp.maximum(nvalid_e - 1, 0)])
    gb0 = jnp.arange(n_gblk, dtype=I32) * tg
    gb_valid = gb0 < gend[-1]
    gb_e = be[gb0 // te]
    rank0 = gb0 - gstart[gb_e]
    cnt_e = counts[gb_e]
    rank_last = jnp.maximum(jnp.minimum(rank0 + tg, cnt_e) - 1, 0)
    cum_cols = cum_e[:, gb_e]
    lo = jnp.sum(cum_cols <= jnp.minimum(rank0, rank_last)[None, :], axis=0) - 1
    hi = jnp.sum(cum_cols <= rank_last[None, :], axis=0) - 1
    lo = jnp.where(gb_valid, lo, 0)
    n_it = jnp.where(gb_valid, hi - lo + 1, 1)
    it_end = jnp.cumsum(n_it)
    it_start = it_end - n_it
    total = it_end[-1]
    ii = jnp.arange(n_items, dtype=I32)
    live = ii < total
    iic = jnp.minimum(ii, total - 1)
    it_j = jnp.sum(it_end[None, :] <= iic[:, None], axis=1).astype(I32)
    it_s = (lo[it_j] + iic - it_start[it_j]).astype(I32)
    it_first = (iic == it_start[it_j]) & live
    comp = live & gb_valid[it_j]
    it_fl = (comp.astype(I32) + 2 * it_first.astype(I32)).astype(I32)
    order = jnp.argsort(jnp.where(comp, it_s, nb), stable=True)
    cj = it_j[order]
    cs = it_s[order]
    cl = comp[order]
    last = jnp.maximum(jnp.sum(comp.astype(I32)) - 1, 0)
    cs = jnp.where(cl, cs, cs[last])
    cj = jnp.where(cl, cj, cj[last])
    c_first = cl & jnp.concatenate([jnp.ones((1,), bool), cs[1:] != cs[:-1]])
    c_fl = (cl.astype(I32) + 2 * c_first.astype(I32)).astype(I32)

    row3 = lambda v: v.reshape(nb, 1, ts)

    gspec = pltpu.PrefetchScalarGridSpec(
        num_scalar_prefetch=3,
        grid=(n_items,),
        in_specs=[pl.BlockSpec((ts, D_MODEL), lambda i, ij, is_, fl: (is_[i], 0))]
        + [pl.BlockSpec((1, 1, ts), lambda i, ij, is_, fl: (is_[i], 0, 0)) for _ in range(4)],
        out_specs=[pl.BlockSpec((tg, D_MODEL), lambda i, ij, is_, fl: (ij[i], 0)),
                   pl.BlockSpec((tg, 1), lambda i, ij, is_, fl: (ij[i], 0))],
    )
    xs, gs = pl.pallas_call(
        functools.partial(_gather_kernel, tg=tg, ts=ts),
        grid_spec=gspec,
        out_shape=[jax.ShapeDtypeStruct((ns, D_MODEL), BF16), jax.ShapeDtypeStruct((ns, 1), F32)],
        compiler_params=_cparams("arbitrary"),
        name="moe_gather",
    )(it_j, it_s, it_fl, hn, row3(pos0), row3(pos1), row3(g1), row3(g2))

    espec = pltpu.PrefetchScalarGridSpec(
        num_scalar_prefetch=2,
        grid=(n_eblk, dffe // tf),
        in_specs=[pl.BlockSpec((te, D_MODEL), lambda j, f, be_, nv: (j, 0)),
                  pl.BlockSpec((te, 1), lambda j, f, be_, nv: (j, 0)),
                  pl.BlockSpec((None, D_MODEL, tf), lambda j, f, be_, nv: (be_[j], 0, f)),
                  pl.BlockSpec((None, D_MODEL, tf), lambda j, f, be_, nv: (be_[j], 0, f)),
                  pl.BlockSpec((None, tf, D_MODEL), lambda j, f, be_, nv: (be_[j], f, 0))],
        out_specs=pl.BlockSpec((te, D_MODEL), lambda j, f, be_, nv: (j, 0)),
        scratch_shapes=[pltpu.VMEM((te, D_MODEL), F32)],
    )
    y = pl.pallas_call(
        _expert_kernel,
        grid_spec=espec,
        out_shape=jax.ShapeDtypeStruct((ns, D_MODEL), BF16),
        compiler_params=_cparams("parallel", "arbitrary"),
        name="moe_experts",
    )(be, nvalid_e.reshape(1), xs, gs, wg, wu, wd)

    cspec = pltpu.PrefetchScalarGridSpec(
        num_scalar_prefetch=3,
        grid=(n_items,),
        in_specs=[pl.BlockSpec((ts, D_MODEL), lambda i, cj_, cs_, fl: (cs_[i], 0)),
                  pl.BlockSpec((tg, D_MODEL), lambda i, cj_, cs_, fl: (cj_[i], 0)),
                  pl.BlockSpec((1, 1, ts), lambda i, cj_, cs_, fl: (cs_[i], 0, 0)),
                  pl.BlockSpec((1, 1, ts), lambda i, cj_, cs_, fl: (cs_[i], 0, 0))],
        out_specs=pl.BlockSpec((ts, D_MODEL), lambda i, cj_, cs_, fl: (cs_[i], 0)),
    )
    return pl.pallas_call(
        functools.partial(_combine_kernel, tg=tg, ts=ts),
        grid_spec=cspec,
        out_shape=jax.ShapeDtypeStruct((t, D_MODEL), F32),
        compiler_params=_cparams("arbitrary"),
        name="moe_combine",
    )(cj, cs, c_fl, x, y, row3(pos0), row3(pos1))


def _layer(x, h, b, l, li, p, mem_k, mem_v, past, kv_stack=None):
    if h is None:
        h = _rms(x, p["norm1"])
    w_in = p["w_in"]
    seg = lambda a: w_in[:, SPLITS[a]:(SPLITS[a + 1] if a + 1 < len(SPLITS) else None)]
    tile = lambda g, n: jnp.tile(g.astype(F32), n)
    (q,) = _mm(h, seg(0), epi="seg64", gain=tile(p["q_norm_a"], 2 * H_A) * (DH_A ** -0.5 * LOG2E),
               layouts=("head",), bl=(b, l))
    k_stack = None if kv_stack is None else (li, kv_stack[0], kv_stack[1])
    v_stack = None if kv_stack is None else (li, kv_stack[0], kv_stack[2])
    tok = "tok" if kv_stack is None else "stack"
    k32, kbf = _mm(h, seg(1), epi="seg64", gain=tile(p["k_norm_a"], 2 * H_A), out_dtypes=(F32, BF16),
                   layouts=(tok, "head"), bl=(b, l), stack=k_stack)
    v32, vbf = _mm(h, seg(2), out_dtypes=(F32, BF16), layouts=(tok, "head"), bl=(b, l), stack=v_stack)
    (u_tm,) = _mm(h, seg(3), out_dtypes=(F32,), layouts=("time",), bl=(b, l))
    (qm,) = _mm(h, _pad_heads(seg(4)), epi="pad192", gain=_pad_heads(tile(p["q_norm_m"], H_M) * (DH_M ** -0.5)))
    (gates,) = _mm(h, seg(5), epi="sigmoid", tn=1024)

    lam_init = 0.8 - 0.6 * math.exp(-0.3 * li)
    lam = (jnp.exp(jnp.sum(p["lam_q1"].astype(F32) * p["lam_k1"].astype(F32)))
           - jnp.exp(jnp.sum(p["lam_q2"].astype(F32) * p["lam_k2"].astype(F32))) + lam_init)
    if past is None:
        o_a = _attn_prompt(q, kbf, vbf, p["rel_bias"], p["subln_a"], lam, lam_init, b, l)
        zero = jnp.zeros((b, N_GROUPS, STATE_P), F32)
        y_b, s_re, s_im = _s5(u_tm, p, zero, zero, b, l)
    else:
        cache_k, cache_v, s0_re, s0_im = past
        o_a = _attn_sample(q, kbf, vbf, cache_k, cache_v, li, p["rel_bias"], p["subln_a"], lam, lam_init, b, l)
        y_b, s_re, s_im = _s5(u_tm, p, s0_re, s0_im, b, l)
    o_m = _memattn(qm, mem_k, mem_v, b, l)
    w_om_pad = _pad_heads(p["w_om"].T).T
    x, hn = _merge(x, o_a, y_b, o_m, gates, p["w_oa"], p["w_ob"], w_om_pad, p["w_out"], p["norm2"], b, l)
    if li % 2 == 0:
        x, h_next = _ffn(x, hn, p["ffn_w_gate"], p["ffn_w_up"], p["ffn_w_down"], p["next_norm1"])
    else:
        x = _moe(x, hn, p["router"], p["moe_w_gate"], p["moe_w_up"], p["moe_w_down"])
        h_next = None
    return x, h_next, k32, v32, s_re, s_im


def kernel(x_prompt, x_sample, mem_prompt, cache_attn_k, cache_attn_v, state_ssm_re, state_ssm_im, cache_mem_k, cache_mem_v, norm1, norm2, w_in, q_norm_a, k_norm_a, lam_q1, lam_k1, lam_q2, lam_k2, subln_a, w_oa, lambda_re, lambda_im, log_dt, b_re, b_im, c_re, c_im, d_skip, w_glu, b_glu, w_ob, w_mk, w_mv, q_norm_m, k_norm_m, w_om, w_out, rel_bias, ffn_w_gate, ffn_w_up, ffn_w_down, router, moe_w_gate, moe_w_up, moe_w_down):
    depth = w_in.shape[0]
    bp, lp, _ = x_prompt.shape
    bs, ls, _ = x_sample.shape
    n_mem = mem_prompt.shape[1]
    xp = x_prompt.reshape(bp * lp, D_MODEL)
    xs = x_sample.reshape(bs * ls, D_MODEL)
    mem_bf = mem_prompt.reshape(bp * n_mem, D_MODEL).astype(BF16)
    outs = {name: [] for name in ("srp", "sip", "mkp", "mvp", "ks", "vs", "srs", "sis")}
    kp_buf = vp_buf = hp = hs = None
    for li in range(depth):
        j = li // 2
        p = {
            "norm1": norm1[li], "next_norm1": norm1[min(li + 1, depth - 1)], "norm2": norm2[li],
            "w_in": w_in[li].astype(BF16),
            "q_norm_a": q_norm_a[li], "k_norm_a": k_norm_a[li],
            "lam_q1": lam_q1[li], "lam_k1": lam_k1[li], "lam_q2": lam_q2[li], "lam_k2": lam_k2[li],
            "subln_a": subln_a[li], "w_oa": w_oa[li].astype(BF16),
            "lambda_re": lambda_re[li], "lambda_im": lambda_im[li], "log_dt": log_dt[li],
            "b_re": b_re[li], "b_im": b_im[li], "c_re": c_re[li], "c_im": c_im[li],
            "d_skip": d_skip[li], "w_glu": w_glu[li], "b_glu": b_glu[li], "w_ob": w_ob[li].astype(BF16),
            "q_norm_m": q_norm_m[li], "w_om": w_om[li].astype(BF16), "w_out": w_out[li].astype(BF16),
            "rel_bias": rel_bias,
        }
        if li % 2 == 0:
            p.update(ffn_w_gate=ffn_w_gate[j].astype(BF16), ffn_w_up=ffn_w_up[j].astype(BF16),
                     ffn_w_down=ffn_w_down[j].astype(BF16))
        else:
            p.update(router=router[j], moe_w_gate=moe_w_gate[j].astype(BF16), moe_w_up=moe_w_up[j].astype(BF16),
                     moe_w_down=moe_w_down[j].astype(BF16))
        mk32, mkbf = _mm(mem_bf, w_mk[li].astype(BF16), epi="seg192", gain=jnp.tile(k_norm_m[li].astype(F32), H_M),
                         out_dtypes=(F32, BF16))
        mv32, mvbf = _mm(mem_bf, w_mv[li].astype(BF16), out_dtypes=(F32, BF16))
        xp, hp, kp_buf, vp_buf, sr_p, si_p = _layer(xp, hp, bp, lp, li, p, _pad_heads(mkbf.reshape(bp, n_mem, W_M)),
                                                    _pad_heads(mvbf.reshape(bp, n_mem, W_M)), None,
                                                    kv_stack=(depth, kp_buf, vp_buf))
        past = (cache_attn_k, cache_attn_v, state_ssm_re[li], state_ssm_im[li])
        xs, hs, k_s, v_s, sr_s, si_s = _layer(xs, hs, bs, ls, li, p,
                                              _pad_heads(cache_mem_k[li].reshape(bs, n_mem, W_M).astype(BF16)),
                                              _pad_heads(cache_mem_v[li].reshape(bs, n_mem, W_M).astype(BF16)), past)
        outs["srp"].append(sr_p)
        outs["sip"].append(si_p)
        outs["mkp"].append(mk32.reshape(bp, n_mem, H_M, DH_M))
        outs["mvp"].append(mv32.reshape(bp, n_mem, H_M, DH_M))
        outs["ks"].append(k_s.reshape(bs, ls, H_A, DV_A))
        outs["vs"].append(v_s.reshape(bs, ls, H_A, DV_A))
        outs["srs"].append(sr_s)
        outs["sis"].append(si_s)
    st = lambda name: jnp.stack(outs[name])
    kv_shape = (depth, bp, lp, H_A, DV_A)
    return (xp.reshape(bp, lp, D_MODEL), xs.reshape(bs, ls, D_MODEL),
            kp_buf.reshape(kv_shape), vp_buf.reshape(kv_shape), st("srp"), st("sip"),
            st("mkp"), st("mvp"), st("ks"), st("vs"), st("srs"), st("sis"))
```

```python
import functools
import math

import jax
import jax.numpy as jnp
from jax import lax
from jax.experimental import pallas as pl
from jax.experimental.pallas import tpu as pltpu

F32 = jnp.float32
BF16 = jnp.bfloat16
I32 = jnp.int32

D_MODEL = 1024
CHUNK = 64
H_A = 8
DH_A = 64
DV_A = 2 * DH_A
W_A = H_A * DV_A
GROUP_CH = 16
N_GROUPS = 48
STATE_P = 64
W_B = N_GROUPS * GROUP_CH
H_M = 4
DH_M = 192
W_M = H_M * DH_M
DH_M_PAD = 256
W_M_PAD = H_M * DH_M_PAD
REL_BUCKETS = 32
REL_MAX_DIST = 128
N_EXPERTS = 8
ROUTER_FIELDS = 8
EPS = 1e-6
NEG_INF = -1e30
LOG2E = math.log2(math.e)
Q_A_W = H_A * 2 * DH_A
SPLITS = (0, Q_A_W, 2 * Q_A_W, 2 * Q_A_W + W_A, 2 * Q_A_W + W_A + W_B, 2 * Q_A_W + W_A + W_B + W_M)
GATE_W = 3 * D_MODEL

LANES = 128
VMEM_LIMIT = 56 * 1024 * 1024
ATTN_GROUP = 4
S5_SLAB_GROUPS = LANES // GROUP_CH
S5_SLABS = N_GROUPS // S5_SLAB_GROUPS
S5_SLAB_STATE = S5_SLAB_GROUPS * STATE_P


def _cparams(*sem):
    return pltpu.CompilerParams(dimension_semantics=sem, vmem_limit_bytes=VMEM_LIMIT)


def _rms_kernel(x_ref, g_ref, o_ref):
    x = x_ref[...]
    ms = jnp.mean(x * x, axis=-1, keepdims=True)
    o_ref[...] = ((x * lax.rsqrt(ms + EPS)) * g_ref[...]).astype(o_ref.dtype)


def _rms(x, g):
    t, d = x.shape
    tm = min(1024, t)
    return pl.pallas_call(
        _rms_kernel,
        grid=(t // tm,),
        in_specs=[pl.BlockSpec((tm, d), lambda i: (i, 0)), pl.BlockSpec((1, d), lambda i: (0, 0))],
        out_specs=pl.BlockSpec((tm, d), lambda i: (i, 0)),
        out_shape=jax.ShapeDtypeStruct((t, d), BF16),
        compiler_params=_cparams("parallel"),
        name="rmsnorm",
    )(x, g.reshape(1, d).astype(F32))


def _seg64_scale(ys):
    lane = lax.broadcasted_iota(I32, (1, LANES), 1)
    left = lane < DH_A
    y2 = ys * ys
    sl = jnp.sum(jnp.where(left, y2, 0.0), axis=-1, keepdims=True)
    sr = jnp.sum(jnp.where(left, 0.0, y2), axis=-1, keepdims=True)
    rl = lax.rsqrt(sl * (1.0 / DH_A) + EPS)
    rr = lax.rsqrt(sr * (1.0 / DH_A) + EPS)
    return jnp.where(left, rl, rr)


def _seg192_scale(y):
    col = lax.broadcasted_iota(I32, (1, W_M), 1)
    y2 = y * y
    rb = jnp.zeros_like(y)
    for h in range(H_M):
        m = (col >= DH_M * h) & (col < DH_M * (h + 1))
        s = jnp.sum(jnp.where(m, y2, 0.0), axis=-1, keepdims=True)
        rb = jnp.where(m, lax.rsqrt(s / DH_M + EPS), rb)
    return rb


def _mm_kernel(*refs, epi, layouts, has_gain, fill_li=None):
    a_ref, w_ref = refs[0], refs[1]
    g_ref = refs[2] if has_gain else None
    outs = list(refs[len(refs) - len(layouts):])
    for k, lay in enumerate(layouts):
        if lay == "stack_fill":
            o = outs[k]
            for d in range(o.shape[0]):
                if d != fill_li:
                    o[d] = jnp.zeros(o.shape[1:], o.dtype)
            outs[k] = o.at[fill_li]
    y = jnp.dot(a_ref[...], w_ref[...], preferred_element_type=F32)
    tn = y.shape[1]
    if epi == "seg192":
        y = (y * _seg192_scale(y)) * g_ref[...]
    elif epi == "pad192":
        for h in range(tn // DH_M_PAD):
            sl = slice(h * DH_M_PAD, (h + 1) * DH_M_PAD)
            ys = y[:, sl]
            r = lax.rsqrt(jnp.sum(ys * ys, axis=-1, keepdims=True) / DH_M + EPS)
            for o in outs:
                o[:, sl] = ((ys * r) * g_ref[:, sl]).astype(o.dtype)
        return
    elif epi == "sigmoid":
        y = jax.nn.sigmoid(y)
    if epi == "seg64" or "head" in layouts:
        for s in range(tn // LANES):
            sl = slice(s * LANES, (s + 1) * LANES)
            ys = y[:, sl]
            if epi == "seg64":
                ys = (ys * _seg64_scale(ys)) * g_ref[:, sl]
            for o, lay in zip(outs, layouts):
                if lay == "head":
                    o[s] = ys.astype(o.dtype)
                else:
                    o[:, sl] = ys.astype(o.dtype)
        return
    for o in outs:
        o[...] = y.astype(o.dtype)


def _mm(a, w, *, epi="plain", gain=None, out_dtypes=(BF16,), layouts=None, tn=None, bl=None, stack=None):
    t, k = a.shape
    n = w.shape[1]
    tn = n if tn is None else tn
    layouts = ("tok",) * len(out_dtypes) if layouts is None else layouts
    assert n % tn == 0 and tn % LANES == 0
    if bl is None:
        assert all(lay in ("tok", "stack") for lay in layouts)
        tm, nl = min(1024, t), 1
    else:
        b, l = bl
        tm = min(1024, l)
        assert l % tm == 0
        nl = l // tm
    assert t % tm == 0
    out_specs, out_shapes = [], []
    alias_args, aliases = [], {}
    n_in = 2 + int(gain is not None)
    for oi, (dt, lay) in enumerate(zip(out_dtypes, layouts)):
        if lay == "tok":
            out_specs.append(pl.BlockSpec((tm, tn), lambda i, j: (i, j)))
            out_shapes.append(jax.ShapeDtypeStruct((t, n), dt))
        elif lay == "stack":
            li, depth, buf = stack
            out_shapes.append(jax.ShapeDtypeStruct((depth, t, n), dt))
            if buf is None:
                out_specs.append(pl.BlockSpec((depth, tm, tn), lambda i, j: (0, i, j)))
                layouts = tuple("stack_fill" if k == oi else lay_k for k, lay_k in enumerate(layouts))
            else:
                out_specs.append(pl.BlockSpec((None, tm, tn), lambda i, j: (li, i, j)))
                aliases[n_in + len(alias_args)] = oi
                alias_args.append(buf)
        elif lay == "time":
            assert tn == n
            out_specs.append(pl.BlockSpec((tm, tn), lambda i, j: (i % nl, i // nl)))
            out_shapes.append(jax.ShapeDtypeStruct((l, b * n), dt))
        else:
            out_specs.append(pl.BlockSpec((None, tn // LANES, tm, LANES), lambda i, j: (i // nl, j, i % nl, 0)))
            out_shapes.append(jax.ShapeDtypeStruct((b, n // LANES, l, LANES), dt))
    in_specs = [pl.BlockSpec((tm, k), lambda i, j: (i, 0)), pl.BlockSpec((k, tn), lambda i, j: (0, j))]
    args = [a, w]
    if gain is not None:
        in_specs.append(pl.BlockSpec((1, tn), lambda i, j: (0, j)))
        args.append(gain.reshape(1, n).astype(F32))
    in_specs += [pl.BlockSpec(memory_space=pl.ANY) for _ in alias_args]
    return pl.pallas_call(
        functools.partial(_mm_kernel, epi=epi, layouts=tuple(layouts), has_gain=gain is not None,
                          fill_li=None if stack is None else stack[0]),
        grid=(t // tm, n // tn),
        in_specs=in_specs,
        out_specs=out_specs,
        out_shape=out_shapes,
        input_output_aliases=aliases,
        compiler_params=_cparams("parallel", "parallel"),
        name="mm_" + epi,
    )(*args, *alias_args)


def _rel_bias(q_pos, k_pos, table):
    rel = k_pos[None, :] - q_pos[:, None]
    half = REL_BUCKETS // 2
    max_exact = half // 2
    n = jnp.abs(rel)
    nf = jnp.maximum(n, 1).astype(F32)
    large = max_exact + (jnp.log(nf / max_exact) / math.log(REL_MAX_DIST / max_exact)
                         * (half - max_exact)).astype(I32)
    large = jnp.minimum(large, half - 1)
    bucket = jnp.where(rel > 0, half, 0) + jnp.where(n < max_exact, n, large)
    tab = table.astype(F32)
    bias = jnp.zeros((H_A,) + bucket.shape, F32)
    for bk in range(REL_BUCKETS):
        bias = jnp.where((bucket == bk)[None], tab[bk][:, None, None], bias)
    visible = (k_pos[None, :] // CHUNK) <= (q_pos[:, None] // CHUNK)
    return jnp.where(visible[None], bias, NEG_INF)


def _split_halves(q):
    lane = lax.broadcasted_iota(I32, (1, LANES), 1)
    zero = jnp.zeros_like(q)
    return jnp.concatenate([jnp.where(lane < DH_A, q, zero), jnp.where(lane >= DH_A, q, zero)], axis=0)


def _subln(o, g, lam_init):
    ms = jnp.mean(o * o, axis=-1, keepdims=True)
    return ((o * lax.rsqrt(ms + EPS)) * g) * (1.0 - lam_init)


def _attn_prompt_kernel(lam_ref, q_ref, k_ref, v_ref, bias_ref, g_ref, o_ref,
                        s_sc, m_sc, acc_sc, *, tq, lam_init):
    qi = pl.program_id(2)
    n_blk = qi + 1
    qs = _split_halves(q_ref[...])
    nt = (((1,), (1,)), ((), ()))
    nc = tq // LANES

    def fold(op, s):
        r = s[:, :LANES]
        for c in range(1, nc):
            r = op(r, s[:, c * LANES:(c + 1) * LANES])
        return r

    def score(j):
        r0 = pl.multiple_of(j * tq, tq)
        bias = bias_ref[jnp.clip(j - (qi - 2), 0, 2)]
        s = lax.dot_general(qs, k_ref[pl.ds(r0, tq), :], nt, preferred_element_type=F32)
        s = s + jnp.concatenate([bias, bias], axis=0)
        s_sc[j] = s
        return fold(jnp.maximum, s)

    def sweep(group_fn):
        n_full = n_blk // ATTN_GROUP

        def trip(g, c):
            group_fn(ATTN_GROUP * g, ATTN_GROUP)
            return c

        lax.fori_loop(0, n_full, trip, 0)
        rem = n_blk - ATTN_GROUP * n_full

        @pl.when(rem >= 2)
        def _():
            group_fn(ATTN_GROUP * n_full, 2)

        @pl.when(rem % 2 == 1)
        def _():
            group_fn(qi, 1)

    m_sc[...] = jnp.full(m_sc.shape, -jnp.inf, F32)

    def score_group(j0, g):
        m_sc[...] = jnp.maximum(m_sc[...], functools.reduce(jnp.maximum, [score(j0 + t) for t in range(g)]))

    sweep(score_group)
    m = jnp.max(m_sc[...], axis=-1, keepdims=True)
    m_sc[...] = jnp.broadcast_to(m, m_sc.shape)
    acc_sc[...] = jnp.zeros(acc_sc.shape, F32)
    ones = jnp.ones((tq, DV_A), BF16)

    def pv(j):
        mrow = m_sc[...]
        s = s_sc[j]
        p = jnp.concatenate([jnp.exp2(s[:, cc * LANES:(cc + 1) * LANES] - mrow) for cc in range(nc)], axis=1)
        r0 = pl.multiple_of(j * tq, tq)
        v_ext = jnp.concatenate([v_ref[pl.ds(r0, tq), :], ones], axis=1)
        return jnp.dot(p.astype(BF16), v_ext, preferred_element_type=F32)

    def pv_group(j0, g):
        acc_sc[...] += functools.reduce(jnp.add, [pv(j0 + t) for t in range(g)])

    sweep(pv_group)
    acc = acc_sc[...]
    o = acc[:, :DV_A] / acc[:, DV_A:]
    o = o[:tq] - lam_ref[0] * o[tq:]
    o_ref[...] = _subln(o, g_ref[...], lam_init).astype(o_ref.dtype)


def _attn_prompt(q, k, v, rel_table, subln_g, lam, lam_init, b, l):
    tq = min(512, l)
    assert l % tq == 0 and tq % CHUNK == 0 and tq >= 128
    cfar = _rel_bias(jnp.array([2 * tq + 1]), jnp.array([0]), rel_table)[:, 0, 0]
    bias = _rel_bias(jnp.arange(tq, 2 * tq), jnp.arange(2 * tq), rel_table)
    bias = (bias - cfar[:, None, None]) * LOG2E
    bias = jnp.stack([jnp.zeros_like(bias[:, :, :tq]), bias[:, :, :tq], bias[:, :, tq:]], axis=1)
    smem = pl.BlockSpec(memory_space=pltpu.SMEM)
    out = pl.pallas_call(
        functools.partial(_attn_prompt_kernel, tq=tq, lam_init=lam_init),
        grid=(b, H_A, l // tq),
        in_specs=[smem,
                  pl.BlockSpec((None, None, tq, DV_A), lambda bi, h, i: (bi, h, i, 0)),
                  pl.BlockSpec((None, None, l, DV_A), lambda bi, h, i: (bi, h, 0, 0)),
                  pl.BlockSpec((None, None, l, DV_A), lambda bi, h, i: (bi, h, 0, 0)),
                  pl.BlockSpec((None, 3, tq, tq), lambda bi, h, i: (h, 0, 0, 0)),
                  pl.BlockSpec((1, DV_A), lambda bi, h, i: (0, 0))],
        out_specs=pl.BlockSpec((None, tq, DV_A), lambda bi, h, i: (bi, i, h)),
        out_shape=jax.ShapeDtypeStruct((b, l, W_A), BF16),
        scratch_shapes=[pltpu.VMEM((l // tq, 2 * tq, tq), F32), pltpu.VMEM((2 * tq, LANES), F32),
                        pltpu.VMEM((2 * tq, 2 * DV_A), F32)],
        compiler_params=_cparams("parallel", "parallel", "arbitrary"),
        name="attn_prompt",
    )(lam.reshape(1), q, k, v, bias, subln_g.reshape(1, DV_A).astype(F32))
    return out.reshape(b * l, W_A)


def _attn_sample_kernel(lam_ref, q_ref, kp_ref, vp_ref, kn_ref, vn_ref, bp_ref, bn_ref, g_ref, o_ref,
                        *, lq, lam_init):
    lam = lam_ref[0]
    nt = (((1,), (1,)), ((), ()))
    for h in range(H_A):
        sl = slice(h * DV_A, (h + 1) * DV_A)
        qs = _split_halves(q_ref[h])
        kp = kp_ref[:, h, :].astype(BF16)
        vp = vp_ref[:, h, :].astype(BF16)
        bp = bp_ref[h]
        bn = bn_ref[h]
        sp = lax.dot_general(qs, kp, nt, preferred_element_type=F32) + jnp.concatenate([bp, bp], axis=0)
        sn = lax.dot_general(qs, kn_ref[h], nt, preferred_element_type=F32) + jnp.concatenate([bn, bn], axis=0)
        m = jnp.maximum(jnp.max(sp, axis=-1, keepdims=True), jnp.max(sn, axis=-1, keepdims=True))
        pp = jnp.exp2(sp - m)
        pn = jnp.exp2(sn - m)
        lsum = jnp.sum(pp, axis=-1, keepdims=True) + jnp.sum(pn, axis=-1, keepdims=True)
        o = (jnp.dot(pp.astype(BF16), vp, preferred_element_type=F32)
             + jnp.dot(pn.astype(BF16), vn_ref[h], preferred_element_type=F32)) / lsum
        o = o[:lq] - lam * o[lq:]
        o_ref[:, sl] = _subln(o, g_ref[...], lam_init).astype(o_ref.dtype)


def _attn_sample(q, k_new, v_new, cache_k, cache_v, li, rel_table, subln_g, lam, lam_init, b, l):
    past = cache_k.shape[2]
    k_pos = jnp.arange(past + l)
    bias = _rel_bias(k_pos[past:], k_pos, rel_table) * LOG2E
    bp, bn = bias[:, :, :past], bias[:, :, past:]
    smem = pl.BlockSpec(memory_space=pltpu.SMEM)
    tok = lambda rows: pl.BlockSpec((None, rows, W_A), lambda bi: (bi, 0, 0))
    head = pl.BlockSpec((None, H_A, l, DV_A), lambda bi: (bi, 0, 0, 0))
    cache = pl.BlockSpec((None, None, past, H_A, DV_A), lambda bi: (li, bi, 0, 0, 0))
    full = lambda shape: pl.BlockSpec(shape, lambda bi: tuple(0 for _ in shape))
    out = pl.pallas_call(
        functools.partial(_attn_sample_kernel, lq=l, lam_init=lam_init),
        grid=(b,),
        in_specs=[smem, head, cache, cache, head, head,
                  full((H_A, l, past)), full((H_A, l, l)), full((1, DV_A))],
        out_specs=tok(l),
        out_shape=jax.ShapeDtypeStruct((b, l, W_A), BF16),
        compiler_params=_cparams("parallel"),
        name="attn_sample",
    )(lam.reshape(1), q, cache_k, cache_v, k_new, v_new, bp, bn, subln_g.reshape(1, DV_A).astype(F32))
    return out.reshape(b * l, W_A)


def _pad_heads(z):
    lead = z.shape[:-1]
    z = z.reshape(*lead, H_M, DH_M)
    z = jnp.pad(z, [(0, 0)] * len(lead) + [(0, 0), (0, DH_M_PAD - DH_M)])
    return z.reshape(*lead, W_M_PAD)


def _memattn_kernel(q_ref, mk_ref, mv_ref, o_ref):
    nt = (((1,), (1,)), ((), ()))
    for h in range(H_M):
        sl = slice(h * DH_M_PAD, (h + 1) * DH_M_PAD)
        s = lax.dot_general(q_ref[:, sl], mk_ref[:, sl], nt, preferred_element_type=F32)
        m = jnp.max(s, axis=-1, keepdims=True)
        p = jnp.exp(s - m)
        p = p / jnp.sum(p, axis=-1, keepdims=True)
        o_ref[:, sl] = jnp.dot(p.astype(BF16), mv_ref[:, sl], preferred_element_type=F32).astype(o_ref.dtype)


def _memattn(qm, mk, mv, b, l):
    n_mem = mk.shape[1]
    tq = min(512, l)
    assert l % tq == 0
    out = pl.pallas_call(
        _memattn_kernel,
        grid=(b, l // tq),
        in_specs=[pl.BlockSpec((None, tq, W_M_PAD), lambda bi, i: (bi, i, 0)),
                  pl.BlockSpec((None, n_mem, W_M_PAD), lambda bi, i: (bi, 0, 0)),
                  pl.BlockSpec((None, n_mem, W_M_PAD), lambda bi, i: (bi, 0, 0))],
        out_specs=pl.BlockSpec((None, tq, W_M_PAD), lambda bi, i: (bi, i, 0)),
        out_shape=jax.ShapeDtypeStruct((b, l, W_M_PAD), BF16),
        compiler_params=_cparams("parallel", "parallel"),
        name="memattn",
    )(qm.reshape(b, l, W_M_PAD), mk, mv)
    return out.reshape(b * l, W_M_PAD)


def _s5_tables(p):
    dt = jnp.exp(p["log_dt"].astype(F32))[:, None]
    lr = jnp.minimum(p["lambda_re"].astype(F32), -1e-4)
    lim = p["lambda_im"].astype(F32)
    mag = jnp.exp(lr * dt)
    ar = mag * jnp.cos(lim * dt)
    ai = mag * jnp.sin(lim * dt)
    den = lr * lr + lim * lim
    fr = ((ar - 1.0) * lr + ai * lim) / den
    fi = (ai * lr - (ar - 1.0) * lim) / den
    br = p["b_re"].astype(F32)
    bi = p["b_im"].astype(F32)
    bbr = fr[..., None] * br - fi[..., None] * bi
    bbi = fr[..., None] * bi + fi[..., None] * br
    eye = jnp.eye(S5_SLAB_GROUPS, dtype=F32)
    sg = (S5_SLABS, S5_SLAB_GROUPS)

    def in_w(bb):
        w = jnp.einsum("sgpc,gh->sgchp", bb.reshape(*sg, STATE_P, GROUP_CH), eye)
        return w.reshape(S5_SLABS, LANES, S5_SLAB_STATE)

    def out_w(c):
        w = jnp.einsum("sgcp,gh->sgphc", c.reshape(*sg, GROUP_CH, STATE_P), eye)
        return w.reshape(S5_SLABS, S5_SLAB_STATE, LANES)

    wb = jnp.concatenate([in_w(bbr), in_w(bbi)], axis=2).astype(BF16)
    wc = jnp.concatenate([out_w(p["c_re"].astype(F32)), -out_w(p["c_im"].astype(F32))], axis=1).astype(BF16)
    a_re = ar.reshape(S5_SLABS, 1, S5_SLAB_STATE)
    a_im = ai.reshape(S5_SLABS, 1, S5_SLAB_STATE)
    return wb, wc, a_re, a_im


def _s5_kernel(u_ref, wb_ref, wc_ref, ar_ref, ai_ref, d_ref, wglu_ref, bglu_ref, s0r_ref, s0i_ref,
               y_ref, sr_ref, si_ref, bu_sc, x_sc, y_sc, *, tl, nb):
    i = pl.program_id(0)

    @pl.when(i == 0)
    def _():
        sr_ref[...] = s0r_ref[...]
        si_ref[...] = s0i_ref[...]

    rows = tl * nb
    u = u_ref[...].reshape(rows, W_B)
    y_sc[...] = d_ref[...] * u
    for s in range(S5_SLABS):
        sl = slice(s * LANES, (s + 1) * LANES)
        bu_sc[...] = jnp.dot(u[:, sl].astype(BF16), wb_ref[s], preferred_element_type=F32)
        ar = jnp.broadcast_to(ar_ref[s], (nb, S5_SLAB_STATE))
        ai = jnp.broadcast_to(ai_ref[s], (nb, S5_SLAB_STATE))

        def body(t, carry):
            sr, si = carry
            r0 = pl.multiple_of(t * nb, nb)
            b_r = bu_sc[pl.ds(r0, nb), :S5_SLAB_STATE]
            b_i = bu_sc[pl.ds(r0, nb), S5_SLAB_STATE:]
            nr = ar * sr - ai * si + b_r
            ni = ar * si + ai * sr + b_i
            x_sc[pl.ds(r0, nb), :S5_SLAB_STATE] = nr.astype(BF16)
            x_sc[pl.ds(r0, nb), S5_SLAB_STATE:] = ni.astype(BF16)
            return nr, ni

        sr, si = lax.fori_loop(0, tl, body, (sr_ref[s], si_ref[s]), unroll=True)
        sr_ref[s] = sr
        si_ref[s] = si
        y_sc[:, sl] = y_sc[:, sl] + jnp.dot(x_sc[...], wc_ref[s], preferred_element_type=F32)
    yg = jax.nn.gelu(y_sc[...])
    z = jnp.dot(yg.astype(BF16), wglu_ref[...], preferred_element_type=F32) + bglu_ref[...]
    out = yg * jax.nn.sigmoid(z)
    y_ref[...] = out.astype(y_ref.dtype).reshape(tl, nb, W_B)


def _s5(u_tm, p, s0_re, s0_im, b, l):
    assert b % 16 == 0
    tl = min(64, l)
    assert l % tl == 0
    wb, wc, a_re, a_im = _s5_tables(p)
    to_slab = lambda s0: jnp.transpose(s0.astype(F32).reshape(b, S5_SLABS, S5_SLAB_STATE), (1, 0, 2))
    from_slab = lambda st: jnp.transpose(st, (1, 0, 2)).reshape(b, N_GROUPS, STATE_P)
    full = lambda shape: pl.BlockSpec(shape, lambda i: tuple(0 for _ in shape))
    st_shape = (S5_SLABS, b, S5_SLAB_STATE)
    rows = tl * b
    y, sr, si = pl.pallas_call(
        functools.partial(_s5_kernel, tl=tl, nb=b),
        grid=(l // tl,),
        in_specs=[pl.BlockSpec((tl, b, W_B), lambda i: (i, 0, 0)),
                  full(wb.shape), full(wc.shape), full(a_re.shape), full(a_im.shape),
                  full((1, W_B)), full((W_B, W_B)), full((1, W_B)), full(st_shape), full(st_shape)],
        out_specs=[pl.BlockSpec((tl, b, W_B), lambda i: (i, 0, 0)), full(st_shape), full(st_shape)],
        out_shape=[jax.ShapeDtypeStruct((l, b, W_B), BF16),
                   jax.ShapeDtypeStruct(st_shape, F32), jax.ShapeDtypeStruct(st_shape, F32)],
        scratch_shapes=[pltpu.VMEM((rows, 2 * S5_SLAB_STATE), F32),
                        pltpu.VMEM((rows, 2 * S5_SLAB_STATE), BF16),
                        pltpu.VMEM((rows, W_B), F32)],
        compiler_params=_cparams("arbitrary"),
        name="s5_scan",
    )(u_tm.reshape(l, b, W_B), wb, wc, a_re, a_im,
      p["d_skip"].astype(F32).reshape(1, W_B), p["w_glu"].astype(BF16), p["b_glu"].astype(F32).reshape(1, W_B),
      to_slab(s0_re), to_slab(s0_im))
    return y.reshape(l, b * W_B), from_slab(sr), from_slab(si)


def _merge_kernel(x_ref, oa_ref, yb_ref, om_ref, gt_ref, woa_ref, wob_ref, wom_ref, wout_ref, g2_ref,
                  xo_ref, hn_ref):
    pa = jnp.dot(oa_ref[...], woa_ref[...], preferred_element_type=F32)
    pb = jnp.dot(yb_ref[...], wob_ref[...], preferred_element_type=F32)
    pm = jnp.dot(om_ref[...], wom_ref[...], preferred_element_type=F32)
    d = D_MODEL
    mix = (gt_ref[:, :d].astype(F32) * pa + gt_ref[:, d:2 * d].astype(F32) * pb
           + gt_ref[:, 2 * d:].astype(F32) * pm)
    x = x_ref[...] + jnp.dot(mix.astype(BF16), wout_ref[...], preferred_element_type=F32)
    xo_ref[...] = x
    ms = jnp.mean(x * x, axis=-1, keepdims=True)
    hn_ref[...] = ((x * lax.rsqrt(ms + EPS)) * g2_ref[...]).astype(hn_ref.dtype)


def _merge(x, o_a, y_b_tm, o_m, gates, w_oa, w_ob, w_om, w_out, g2, b, l):
    t = b * l
    tm = min(512, l)
    assert l % tm == 0
    nl = l // tm
    row = lambda w: pl.BlockSpec((tm, w), lambda bi, i: (bi * nl + i, 0))
    full = lambda shape: pl.BlockSpec(shape, lambda bi, i: (0, 0))
    return pl.pallas_call(
        _merge_kernel,
        grid=(b, nl),
        in_specs=[row(D_MODEL), row(W_A), pl.BlockSpec((tm, W_B), lambda bi, i: (i, bi)), row(W_M_PAD), row(GATE_W),
                  full(w_oa.shape), full(w_ob.shape), full(w_om.shape), full(w_out.shape), full((1, D_MODEL))],
        out_specs=[row(D_MODEL), row(D_MODEL)],
        out_shape=[jax.ShapeDtypeStruct((t, D_MODEL), F32), jax.ShapeDtypeStruct((t, D_MODEL), BF16)],
        compiler_params=_cparams("parallel", "parallel"),
        name="merge",
    )(x, o_a, y_b_tm, o_m, gates, w_oa, w_ob, w_om, w_out, g2.reshape(1, D_MODEL).astype(F32))


def _ffn_kernel(x_ref, h_ref, wg_ref, wu_ref, wd_ref, gn_ref, o_ref, hn_ref, acc_sc):
    f = pl.program_id(1)

    @pl.when(f == 0)
    def _():
        acc_sc[...] = x_ref[...]

    h = h_ref[...]
    hg = jnp.dot(h, wg_ref[...], preferred_element_type=F32)
    hu = jnp.dot(h, wu_ref[...], preferred_element_type=F32)
    mid = (jax.nn.silu(hg) * hu).astype(BF16)
    acc_sc[...] += jnp.dot(mid, wd_ref[...], preferred_element_type=F32)

    @pl.when(f == pl.num_programs(1) - 1)
    def _():
        y = acc_sc[...]
        o_ref[...] = y
        ms = jnp.mean(y * y, axis=-1, keepdims=True)
        hn_ref[...] = ((y * lax.rsqrt(ms + EPS)) * gn_ref[...]).astype(hn_ref.dtype)


def _ffn(x, hn, wg, wu, wd, next_gain):
    t = x.shape[0]
    dff = wg.shape[1]
    tm = min(512, t)
    tf = dff // 2 if (dff // 2) % LANES == 0 else dff
    return pl.pallas_call(
        _ffn_kernel,
        grid=(t // tm, dff // tf),
        in_specs=[pl.BlockSpec((tm, D_MODEL), lambda i, f: (i, 0)), pl.BlockSpec((tm, D_MODEL), lambda i, f: (i, 0)),
                  pl.BlockSpec((D_MODEL, tf), lambda i, f: (0, f)), pl.BlockSpec((D_MODEL, tf), lambda i, f: (0, f)),
                  pl.BlockSpec((tf, D_MODEL), lambda i, f: (f, 0)), pl.BlockSpec((1, D_MODEL), lambda i, f: (0, 0))],
        out_specs=[pl.BlockSpec((tm, D_MODEL), lambda i, f: (i, 0)), pl.BlockSpec((tm, D_MODEL), lambda i, f: (i, 0))],
        out_shape=[jax.ShapeDtypeStruct((t, D_MODEL), F32), jax.ShapeDtypeStruct((t, D_MODEL), BF16)],
        scratch_shapes=[pltpu.VMEM((tm, D_MODEL), F32)],
        compiler_params=_cparams("parallel", "arbitrary"),
        name="ffn_dense",
    )(x, hn, wg, wu, wd, next_gain.reshape(1, D_MODEL).astype(F32))


def _router_kernel(h_ref, w_ref, tri_ref, meta_ref, cum_ref, tot_ref, carry_sc):
    i = pl.program_id(0)

    @pl.when(i == 0)
    def _():
        carry_sc[...] = jnp.zeros(carry_sc.shape, F32)

    lane = lax.broadcasted_iota(I32, (1, LANES), 1)
    logits = jnp.dot(h_ref[...], w_ref[...], preferred_element_type=F32)
    lg = jnp.where(lane < N_EXPERTS, logits, -jnp.inf)
    m1 = jnp.max(lg, axis=-1, keepdims=True)
    i1 = jnp.min(jnp.where(lg == m1, lane, LANES), axis=-1, keepdims=True)
    lg2 = jnp.where(lane == i1, -jnp.inf, lg)
    m2 = jnp.max(lg2, axis=-1, keepdims=True)
    i2 = jnp.min(jnp.where(lg2 == m2, lane, LANES), axis=-1, keepdims=True)
    e = jnp.exp(m2 - m1)
    g1 = 1.0 / (1.0 + e)
    g2 = e / (1.0 + e)
    hit1 = lane == i1
    hit2 = lane == i2
    cnt = jnp.where(hit1 | hit2, 1.0, 0.0)
    carry = carry_sc[...]
    before = jnp.dot(tri_ref[...], cnt.astype(BF16), preferred_element_type=F32) + carry
    r1 = jnp.sum(jnp.where(hit1, before, 0.0), axis=-1, keepdims=True)
    r2 = jnp.sum(jnp.where(hit2, before, 0.0), axis=-1, keepdims=True)
    cum_ref[0] = carry
    carry = carry + jnp.sum(cnt, axis=0, keepdims=True)
    carry_sc[...] = carry
    tot_ref[...] = carry
    meta = jnp.zeros(logits.shape, F32)
    for c, val in enumerate((i1.astype(F32), i2.astype(F32), g1, g2, r1, r2)):
        meta = jnp.where(lane == c, val, meta)
    meta_ref[0] = meta.T[:ROUTER_FIELDS]


def _gather_kernel(ij_ref, is_ref, ifl_ref, h_ref, p0_ref, p1_ref, g0_ref, g1_ref, xs_ref, gs_ref, *, tg, ts):
    i = pl.program_id(0)
    fl = ifl_ref[i]

    @pl.when((fl & 2) != 0)
    def _():
        xs_ref[...] = jnp.zeros(xs_ref.shape, xs_ref.dtype)
        gs_ref[...] = jnp.zeros(gs_ref.shape, gs_ref.dtype)

    @pl.when((fl & 1) != 0)
    def _():
        rio = lax.broadcasted_iota(I32, (tg, ts), 0) + ij_ref[i] * tg
        m0 = p0_ref[0] == rio
        m1 = p1_ref[0] == rio
        sel = jnp.where(m0 | m1, 1.0, 0.0).astype(BF16)
        xs_ref[...] += jnp.dot(sel, h_ref[...], preferred_element_type=F32).astype(xs_ref.dtype)
        gs_ref[...] += jnp.sum(jnp.where(m0, g0_ref[0], 0.0) + jnp.where(m1, g1_ref[0], 0.0),
                               axis=-1, keepdims=True)


def _gather_ring_kernel(lo_ref, n_ref, h_hbm, p0_ref, p1_ref, g0_ref, g1_ref, xs_ref, gs_ref, buf, sem, *, tg, ts):
    j = pl.program_id(0)
    lo = lo_ref[j]
    n = n_ref[j]

    def copy(s, slot):
        return pltpu.make_async_copy(h_hbm.at[pl.ds(pl.multiple_of(s * ts, ts), ts), :], buf.at[slot], sem.at[slot])

    xs_ref[...] = jnp.zeros(xs_ref.shape, xs_ref.dtype)
    gs_ref[...] = jnp.zeros(gs_ref.shape, gs_ref.dtype)

    @pl.when(n > 0)
    def _():
        copy(lo, 0).start()

    rio = lax.broadcasted_iota(I32, (tg, ts), 0) + j * tg

    def body(k, c):
        slot = k % 2

        @pl.when(k + 1 < n)
        def _():
            copy(lo + k + 1, 1 - slot).start()

        copy(lo + k, slot).wait()
        s = lo + k
        m0 = p0_ref[s] == rio
        m1 = p1_ref[s] == rio
        sel = jnp.where(m0 | m1, 1.0, 0.0).astype(BF16)
        xs_ref[...] += jnp.dot(sel, buf[slot], preferred_element_type=F32).astype(xs_ref.dtype)
        gs_ref[...] += jnp.sum(jnp.where(m0, g0_ref[s], 0.0) + jnp.where(m1, g1_ref[s], 0.0),
                               axis=-1, keepdims=True)
        return c

    lax.fori_loop(0, n, body, 0)


def _expert_kernel(be_ref, nv_ref, xs_ref, gs_ref, wg_ref, wu_ref, wd_ref, y_ref, acc_sc):
    j = pl.program_id(0)
    f = pl.program_id(1)
    valid = j < nv_ref[0]

    @pl.when(f == 0)
    def _():
        acc_sc[...] = jnp.zeros(acc_sc.shape, F32)

    @pl.when(valid)
    def _():
        x = xs_ref[...]
        hg = jnp.dot(x, wg_ref[...], preferred_element_type=F32)
        hu = jnp.dot(x, wu_ref[...], preferred_element_type=F32)
        mid = (jax.nn.silu(hg) * hu).astype(BF16)
        acc_sc[...] += jnp.dot(mid, wd_ref[...], preferred_element_type=F32)

    @pl.when(f == pl.num_programs(1) - 1)
    def _():
        y_ref[...] = jnp.where(valid, acc_sc[...] * gs_ref[...], 0.0).astype(y_ref.dtype)


def _combine_kernel(cj_ref, cs_ref, cfl_ref, x_ref, y_ref, p0_ref, p1_ref, o_ref, *, tg, ts):
    i = pl.program_id(0)
    fl = cfl_ref[i]

    @pl.when((fl & 2) != 0)
    def _():
        o_ref[...] = x_ref[...]

    @pl.when((fl & 1) != 0)
    def _():
        rio = lax.broadcasted_iota(I32, (tg, ts), 0) + cj_ref[i] * tg
        sel = jnp.where((p0_ref[0] == rio) | (p1_ref[0] == rio), 1.0, 0.0).astype(BF16)
        o_ref[...] += lax.dot_general(sel, y_ref[...], (((0,), (0,)), ((), ())), preferred_element_type=F32)


def _moe(x, hn, router_w, wg, wu, wd):
    t = x.shape[0]
    dffe = wg.shape[2]
    ts = min(1024, t)
    tg = min(256, t)
    te = min(1024, t)
    tf = 512
    assert t % ts == 0 and te % tg == 0 and dffe % tf == 0
    nb = t // ts
    n_eblk = (2 * t) // te + N_EXPERTS
    ns = n_eblk * te
    n_gblk = ns // tg
    n_items = n_gblk + N_EXPERTS * nb

    w_pad = jnp.zeros((D_MODEL, LANES), BF16).at[:, :N_EXPERTS].set(router_w.astype(BF16))
    tri = (jnp.arange(ts)[:, None] > jnp.arange(ts)[None, :]).astype(BF16)
    meta, cum, tot = pl.pallas_call(
        _router_kernel,
        grid=(nb,),
        in_specs=[pl.BlockSpec((ts, D_MODEL), lambda i: (i, 0)), pl.BlockSpec((D_MODEL, LANES), lambda i: (0, 0)),
                  pl.BlockSpec((ts, ts), lambda i: (0, 0))],
        out_specs=[pl.BlockSpec((1, ROUTER_FIELDS, ts), lambda i: (i, 0, 0)),
                   pl.BlockSpec((1, 1, LANES), lambda i: (i, 0, 0)), pl.BlockSpec((1, LANES), lambda i: (0, 0))],
        out_shape=[jax.ShapeDtypeStruct((nb, ROUTER_FIELDS, ts), F32), jax.ShapeDtypeStruct((nb, 1, LANES), F32),
                   jax.ShapeDtypeStruct((1, LANES), F32)],
        scratch_shapes=[pltpu.VMEM((1, LANES), F32)],
        compiler_params=_cparams("arbitrary"),
        name="moe_router",
    )(hn, w_pad, tri)

    e1 = meta[:, 0, :].astype(I32)
    e2 = meta[:, 1, :].astype(I32)
    g1 = meta[:, 2, :]
    g2 = meta[:, 3, :]
    r1 = meta[:, 4, :].astype(I32)
    r2 = meta[:, 5, :].astype(I32)
    counts = tot[0, :N_EXPERTS].astype(I32)
    cum_e = cum[:, 0, :N_EXPERTS].astype(I32)
    gpad = ((counts + te - 1) // te) * te
    gend = jnp.cumsum(gpad)
    gstart = gend - gpad

    def group_start(e):
        out = jnp.zeros(e.shape, I32)
        for ex in range(N_EXPERTS):
            out = jnp.where(e == ex, gstart[ex], out)
        return out

    pos0 = group_start(e1) + r1
    pos1 = group_start(e2) + r2
    nvalid_e = (gend[-1] // te).astype(I32)
    blk_e = jnp.arange(n_eblk, dtype=I32) * te
    be = jnp.minimum(jnp.sum(gend[None, :] <= blk_e[:, None], axis=1), N_EXPERTS - 1).astype(I32)
    be = jnp.where(jnp.arange(n_eblk) < nvalid_e, be, be[jnp.maximum(nvalid_e - 1, 0)])
    gb0 = jnp.arange(n_gblk, dtype=I32) * tg
    gb_valid = gb0 < gend[-1]
    gb_e = be[gb0 // te]
    rank0 = gb0 - gstart[gb_e]
    cnt_e = counts[gb_e]
    rank_last = jnp.maximum(jnp.minimum(rank0 + tg, cnt_e) - 1, 0)
    cum_cols = cum_e[:, gb_e]
    lo = jnp.sum(cum_cols <= jnp.minimum(rank0, rank_last)[None, :], axis=0) - 1
    hi = jnp.sum(cum_cols <= rank_last[None, :], axis=0) - 1
    lo = jnp.where(gb_valid, lo, 0)
    n_it = jnp.where(gb_valid, hi - lo + 1, 1)
    it_end = jnp.cumsum(n_it)
    it_start = it_end - n_it
    total = it_end[-1]
    ii = jnp.arange(n_items, dtype=I32)
    live = ii < total
    iic = jnp.minimum(ii, total - 1)
    it_j = jnp.sum(it_end[None, :] <= iic[:, None], axis=1).astype(I32)
    it_s = (lo[it_j] + iic - it_start[it_j]).astype(I32)
    it_first = (iic == it_start[it_j]) & live
    comp = live & gb_valid[it_j]
    it_fl = (comp.astype(I32) + 2 * it_first.astype(I32)).astype(I32)
    order = jnp.argsort(jnp.where(comp, it_s, nb), stable=True)
    cj = it_j[order]
    cs = it_s[order]
    cl = comp[order]
    last = jnp.maximum(jnp.sum(comp.astype(I32)) - 1, 0)
    cs = jnp.where(cl, cs, cs[last])
    cj = jnp.where(cl, cj, cj[last])
    c_first = cl & jnp.concatenate([jnp.ones((1,), bool), cs[1:] != cs[:-1]])
    c_fl = (cl.astype(I32) + 2 * c_first.astype(I32)).astype(I32)

    row3 = lambda v: v.reshape(nb, 1, ts)

    n_src = jnp.where(gb_valid, hi - lo + 1, 0).astype(I32)
    whole = lambda: pl.BlockSpec((nb, 1, ts), lambda jg, lo_, n_: (0, 0, 0))
    gspec = pltpu.PrefetchScalarGridSpec(
        num_scalar_prefetch=2,
        grid=(n_gblk,),
        in_specs=[pl.BlockSpec(memory_space=pl.ANY), whole(), whole(), whole(), whole()],
        out_specs=[pl.BlockSpec((tg, D_MODEL), lambda jg, lo_, n_: (jg, 0)),
                   pl.BlockSpec((tg, 1), lambda jg, lo_, n_: (jg, 0))],
        scratch_shapes=[pltpu.VMEM((2, ts, D_MODEL), BF16), pltpu.SemaphoreType.DMA((2,))],
    )
    xs, gs = pl.pallas_call(
        functools.partial(_gather_ring_kernel, tg=tg, ts=ts),
        grid_spec=gspec,
        out_shape=[jax.ShapeDtypeStruct((ns, D_MODEL), BF16), jax.ShapeDtypeStruct((ns, 1), F32)],
        compiler_params=_cparams("arbitrary"),
        name="moe_gather",
    )(lo.astype(I32), n_src, hn, row3(pos0), row3(pos1), row3(g1), row3(g2))

    espec = pltpu.PrefetchScalarGridSpec(
        num_scalar_prefetch=2,
        grid=(n_eblk, dffe // tf),
        in_specs=[pl.BlockSpec((te, D_MODEL), lambda j, f, be_, nv: (j, 0)),
                  pl.BlockSpec((te, 1), lambda j, f, be_, nv: (j, 0)),
                  pl.BlockSpec((None, D_MODEL, tf), lambda j, f, be_, nv: (be_[j], 0, f)),
                  pl.BlockSpec((None, D_MODEL, tf), lambda j, f, be_, nv: (be_[j], 0, f)),
                  pl.BlockSpec((None, tf, D_MODEL), lambda j, f, be_, nv: (be_[j], f, 0))],
        out_specs=pl.BlockSpec((te, D_MODEL), lambda j, f, be_, nv: (j, 0)),
        scratch_shapes=[pltpu.VMEM((te, D_MODEL), F32)],
    )
    y = pl.pallas_call(
        _expert_kernel,
        grid_spec=espec,
        out_shape=jax.ShapeDtypeStruct((ns, D_MODEL), BF16),
        compiler_params=_cparams("parallel", "arbitrary"),
        name="moe_experts",
    )(be, nvalid_e.reshape(1), xs, gs, wg, wu, wd)

    cspec = pltpu.PrefetchScalarGridSpec(
        num_scalar_prefetch=3,
        grid=(n_items,),
        in_specs=[pl.BlockSpec((ts, D_MODEL), lambda i, cj_, cs_, fl: (cs_[i], 0)),
                  pl.BlockSpec((tg, D_MODEL), lambda i, cj_, cs_, fl: (cj_[i], 0)),
                  pl.BlockSpec((1, 1, ts), lambda i, cj_, cs_, fl: (cs_[i], 0, 0)),
                  pl.BlockSpec((1, 1, ts), lambda i, cj_, cs_, fl: (cs_[i], 0, 0))],
        out_specs=pl.BlockSpec((ts, D_MODEL), lambda i, cj_, cs_, fl: (cs_[i], 0)),
    )
    return pl.pallas_call(
        functools.partial(_combine_kernel, tg=tg, ts=ts),
        grid_spec=cspec,
        out_shape=jax.ShapeDtypeStruct((t, D_MODEL), F32),
        compiler_params=_cparams("arbitrary"),
        name="moe_combine",
    )(cj, cs, c_fl, x, y, row3(pos0), row3(pos1))


def _layer(x, h, b, l, li, p, mem_k, mem_v, past, kv_stack=None):
    if h is None:
        h = _rms(x, p["norm1"])
    w_in = p["w_in"]
    seg = lambda a: w_in[:, SPLITS[a]:(SPLITS[a + 1] if a + 1 < len(SPLITS) else None)]
    tile = lambda g, n: jnp.tile(g.astype(F32), n)
    (q,) = _mm(h, seg(0), epi="seg64", gain=tile(p["q_norm_a"], 2 * H_A) * (DH_A ** -0.5 * LOG2E),
               layouts=("head",), bl=(b, l))
    k_stack = None if kv_stack is None else (li, kv_stack[0], kv_stack[1])
    v_stack = None if kv_stack is None else (li, kv_stack[0], kv_stack[2])
    tok = "tok" if kv_stack is None else "stack"
    k32, kbf = _mm(h, seg(1), epi="seg64", gain=tile(p["k_norm_a"], 2 * H_A), out_dtypes=(F32, BF16),
                   layouts=(tok, "head"), bl=(b, l), stack=k_stack)
    v32, vbf = _mm(h, seg(2), out_dtypes=(F32, BF16), layouts=(tok, "head"), bl=(b, l), stack=v_stack)
    (u_tm,) = _mm(h, seg(3), out_dtypes=(F32,), layouts=("time",), bl=(b, l))
    (qm,) = _mm(h, _pad_heads(seg(4)), epi="pad192", gain=_pad_heads(tile(p["q_norm_m"], H_M) * (DH_M ** -0.5)))
    (gates,) = _mm(h, seg(5), epi="sigmoid", tn=1024)

    lam_init = 0.8 - 0.6 * math.exp(-0.3 * li)
    lam = (jnp.exp(jnp.sum(p["lam_q1"].astype(F32) * p["lam_k1"].astype(F32)))
           - jnp.exp(jnp.sum(p["lam_q2"].astype(F32) * p["lam_k2"].astype(F32))) + lam_init)
    if past is None:
        o_a = _attn_prompt(q, kbf, vbf, p["rel_bias"], p["subln_a"], lam, lam_init, b, l)
        zero = jnp.zeros((b, N_GROUPS, STATE_P), F32)
        y_b, s_re, s_im = _s5(u_tm, p, zero, zero, b, l)
    else:
        cache_k, cache_v, s0_re, s0_im = past
        o_a = _attn_sample(q, kbf, vbf, cache_k, cache_v, li, p["rel_bias"], p["subln_a"], lam, lam_init, b, l)
        y_b, s_re, s_im = _s5(u_tm, p, s0_re, s0_im, b, l)
    o_m = _memattn(qm, mem_k, mem_v, b, l)
    w_om_pad = _pad_heads(p["w_om"].T).T
    x, hn = _merge(x, o_a, y_b, o_m, gates, p["w_oa"], p["w_ob"], w_om_pad, p["w_out"], p["norm2"], b, l)
    if li % 2 == 0:
        x, h_next = _ffn(x, hn, p["ffn_w_gate"], p["ffn_w_up"], p["ffn_w_down"], p["next_norm1"])
    else:
        x = _moe(x, hn, p["router"], p["moe_w_gate"], p["moe_w_up"], p["moe_w_down"])
        h_next = None
    return x, h_next, k32, v32, s_re, s_im


def kernel(x_prompt, x_sample, mem_prompt, cache_attn_k, cache_attn_v, state_ssm_re, state_ssm_im, cache_mem_k, cache_mem_v, norm1, norm2, w_in, q_norm_a, k_norm_a, lam_q1, lam_k1, lam_q2, lam_k2, subln_a, w_oa, lambda_re, lambda_im, log_dt, b_re, b_im, c_re, c_im, d_skip, w_glu, b_glu, w_ob, w_mk, w_mv, q_norm_m, k_norm_m, w_om, w_out, rel_bias, ffn_w_gate, ffn_w_up, ffn_w_down, router, moe_w_gate, moe_w_up, moe_w_down):
    depth = w_in.shape[0]
    bp, lp, _ = x_prompt.shape
    bs, ls, _ = x_sample.shape
    n_mem = mem_prompt.shape[1]
    xp = x_prompt.reshape(bp * lp, D_MODEL)
    xs = x_sample.reshape(bs * ls, D_MODEL)
    mem_bf = mem_prompt.reshape(bp * n_mem, D_MODEL).astype(BF16)
    outs = {name: [] for name in ("srp", "sip", "mkp", "mvp", "ks", "vs", "srs", "sis")}
    kp_buf = vp_buf = hp = hs = None
    for li in range(depth):
        j = li // 2
        p = {
            "norm1": norm1[li], "next_norm1": norm1[min(li + 1, depth - 1)], "norm2": norm2[li],
            "w_in": w_in[li].astype(BF16),
            "q_norm_a": q_norm_a[li], "k_norm_a": k_norm_a[li],
            "lam_q1": lam_q1[li], "lam_k1": lam_k1[li], "lam_q2": lam_q2[li], "lam_k2": lam_k2[li],
            "subln_a": subln_a[li], "w_oa": w_oa[li].astype(BF16),
            "lambda_re": lambda_re[li], "lambda_im": lambda_im[li], "log_dt": log_dt[li],
            "b_re": b_re[li], "b_im": b_im[li], "c_re": c_re[li], "c_im": c_im[li],
            "d_skip": d_skip[li], "w_glu": w_glu[li], "b_glu": b_glu[li], "w_ob": w_ob[li].astype(BF16),
            "q_norm_m": q_norm_m[li], "w_om": w_om[li].astype(BF16), "w_out": w_out[li].astype(BF16),
            "rel_bias": rel_bias,
        }
        if li % 2 == 0:
            p.update(ffn_w_gate=ffn_w_gate[j].astype(BF16), ffn_w_up=ffn_w_up[j].astype(BF16),
                     ffn_w_down=ffn_w_down[j].astype(BF16))
        else:
            p.update(router=router[j], moe_w_gate=moe_w_gate[j].astype(BF16), moe_w_up=moe_w_up[j].astype(BF16),
                     moe_w_down=moe_w_down[j].astype(BF16))
        mk32, mkbf = _mm(mem_bf, w_mk[li].astype(BF16), epi="seg192", gain=jnp.tile(k_norm_m[li].astype(F32), H_M),
                         out_dtypes=(F32, BF16))
        mv32, mvbf = _mm(mem_bf, w_mv[li].astype(BF16), out_dtypes=(F32, BF16))
        xp, hp, kp_buf, vp_buf, sr_p, si_p = _layer(xp, hp, bp, lp, li, p, _pad_heads(mkbf.reshape(bp, n_mem, W_M)),
                                                    _pad_heads(mvbf.reshape(bp, n_mem, W_M)), None,
                                                    kv_stack=(depth, kp_buf, vp_buf))
        past = (cache_attn_k, cache_attn_v, state_ssm_re[li], state_ssm_im[li])
        xs, hs, k_s, v_s, sr_s, si_s = _layer(xs, hs, bs, ls, li, p,
                                              _pad_heads(cache_mem_k[li].reshape(bs, n_mem, W_M).astype(BF16)),
                                              _pad_heads(cache_mem_v[li].reshape(bs, n_mem, W_M).astype(BF16)), past)
        outs["srp"].append(sr_p)
        outs["sip"].append(si_p)
        outs["mkp"].append(mk32.reshape(bp, n_mem, H_M, DH_M))
        outs["mvp"].append(mv32.reshape(bp, n_mem, H_M, DH_M))
        outs["ks"].append(k_s.reshape(bs, ls, H_A, DV_A))
        outs["vs"].append(v_s.reshape(bs, ls, H_A, DV_A))
        outs["srs"].append(sr_s)
        outs["sis"].append(si_s)
    st = lambda name: jnp.stack(outs[name])
    kv_shape = (depth, bp, lp, H_A, DV_A)
    return (xp.reshape(bp, lp, D_MODEL), xs.reshape(bs, ls, D_MODEL),
            kp_buf.reshape(kv_shape), vp_buf.reshape(kv_shape), st("srp"), st("sip"),
            st("mkp"), st("mvp"), st("ks"), st("vs"), st("srs"), st("sis"))
```
